```python
import math
import jax, jax.numpy as jnp
from jax import lax
import numpy as np

D_MODEL = 2048
BATCH = 2
SEQ = 4096
DEPTH = 4
DEC_BATCH = 8
DEC_SEQ = 8
PAST_LEN = 16384
PAGE_SIZE = 128

HEAD_DIM = 128
ATT_W = D_MODEL // 2
N_HEADS = ATT_W // HEAD_DIM
GROUP_CH = 16
SSM_W = D_MODEL // 2
N_GROUPS = SSM_W // GROUP_CH
STATE_N = 64
D_FF = ((8 * D_MODEL // 3 + 255) // 256) * 256
MEM_TOKENS = 256
X_HEADS = 4
X_HEAD_DIM = 128
X_W = X_HEADS * X_HEAD_DIM
Q_BLOCK = 128
EPS = 1e-6
FORGET_BIAS_INIT = 4.0
CACHE_FORGET_LOGIT = 7.0
IN_SPLITS = [ATT_W, 2 * ATT_W, 3 * ATT_W, 3 * ATT_W + N_HEADS,
             3 * ATT_W + N_HEADS + SSM_W, 3 * ATT_W + N_HEADS + SSM_W + D_MODEL]
IN_W = 3 * ATT_W + N_HEADS + SSM_W + 2 * D_MODEL

kernel_name = "fox_s5_macaron_memxattn_step"


def rmsnorm(x, g):
    xf = x.astype(jnp.float32)
    y = xf * lax.rsqrt(jnp.mean(xf * xf, axis=-1, keepdims=True) + EPS)
    return (y * g.astype(jnp.float32)).astype(x.dtype)


def swiglu_half(x, g, w_gate, w_up, w_down):
    h = rmsnorm(x, g)
    return x + 0.5 * ((jax.nn.silu(h @ w_gate) * (h @ w_up)) @ w_down)


def mixer_inputs(u, w_in, b_forget, q_g, k_g):
    B, T, _ = u.shape
    z = u @ w_in
    q, k, v, fl, s, ga, gb = jnp.split(z, IN_SPLITS, axis=-1)
    q = rmsnorm(q.reshape(B, T, N_HEADS, HEAD_DIM), q_g)
    k = rmsnorm(k.reshape(B, T, N_HEADS, HEAD_DIM), k_g)
    v = v.reshape(B, T, N_HEADS, HEAD_DIM)
    logf = jax.nn.log_sigmoid((fl + b_forget).astype(jnp.float32))
    s = s.reshape(B, T, N_GROUPS, GROUP_CH)
    return q, k, v, logf, s, jax.nn.sigmoid(ga), jax.nn.sigmoid(gb)


def fox_prompt(q, k, v, logf):
    B, T, H, Dh = q.shape
    scale = Dh ** -0.5
    c = jnp.cumsum(logf, axis=1)
    nb = T // Q_BLOCK
    qb = q.reshape(B, nb, Q_BLOCK, H, Dh).transpose(1, 0, 2, 3, 4)
    cb = c.reshape(B, nb, Q_BLOCK, H).transpose(1, 0, 2, 3)
    ck = c.transpose(0, 2, 1)[:, :, None, :]
    k_pos = jnp.arange(T)

    def block(args):
        qi, ci, i = args
        s = jnp.einsum('bqhd,bkhd->bhqk', qi, k, preferred_element_type=jnp.float32) * scale
        s = s + ci.transpose(0, 2, 1)[..., None] - ck
        q_pos = i * Q_BLOCK + jnp.arange(Q_BLOCK)
        s = jnp.where(k_pos[None, :] <= q_pos[:, None], s, -jnp.inf)
        p = jax.nn.softmax(s, axis=-1).astype(v.dtype)
        return jnp.einsum('bhqk,bkhd->bqhd', p, v)

    out = lax.map(block, (qb, cb, jnp.arange(nb)))
    return out.transpose(1, 0, 2, 3, 4).reshape(B, T, H * Dh)


def fox_sample(q, k_new, v_new, logf_new, k_past, v_past, logf_past):
    Bd, Tn, H, Dh = q.shape
    P = k_past.shape[1]
    scale = Dh ** -0.5
    logf_past = logf_past.astype(jnp.float32)
    c_new = jnp.cumsum(logf_new, axis=1)
    c_past = logf_past - jnp.cumsum(logf_past[:, ::-1], axis=1)[:, ::-1]
    cq = c_new.transpose(0, 2, 1)
    s_past = jnp.einsum('bqhd,bkhd->bhqk', q, k_past, preferred_element_type=jnp.float32) * scale
    s_past = s_past + cq[..., None] - c_past.transpose(0, 2, 1)[:, :, None, :]
    s_new = jnp.einsum('bqhd,bkhd->bhqk', q, k_new, preferred_element_type=jnp.float32) * scale
    s_new = s_new + cq[..., None] - cq[:, :, None, :]
    causal = jnp.arange(Tn)[None, :] <= jnp.arange(Tn)[:, None]
    s_new = jnp.where(causal, s_new, -jnp.inf)
    p = jax.nn.softmax(jnp.concatenate([s_past, s_new], axis=-1), axis=-1).astype(v_new.dtype)
    out = (jnp.einsum('bhqk,bkhd->bqhd', p[..., :P], v_past)
           + jnp.einsum('bhqk,bkhd->bqhd', p[..., P:], v_new))
    return out.reshape(Bd, Tn, H * Dh)


def _complex_affine_combine(e1, e2):
    a1r, a1i, b1r, b1i = e1
    a2r, a2i, b2r, b2i = e2
    return (a2r * a1r - a2i * a1i, a2r * a1i + a2i * a1r,
            a2r * b1r - a2i * b1i + b2r, a2r * b1i + a2i * b1r + b2i)


def s5_ssm(u, h0_re, h0_im, lam_re, lam_im, log_dt, b_re, b_im, c_re, c_im, d):
    dt = jnp.exp(log_dt)[:, None]
    mag = jnp.exp(lam_re * dt)
    a_re = mag * jnp.cos(lam_im * dt)
    a_im = mag * jnp.sin(lam_im * dt)
    den = lam_re * lam_re + lam_im * lam_im
    z_re = ((a_re - 1) * lam_re + a_im * lam_im) / den
    z_im = (a_im * lam_re - (a_re - 1) * lam_im) / den
    bb_re = z_re[..., None] * b_re - z_im[..., None] * b_im
    bb_im = z_re[..., None] * b_im + z_im[..., None] * b_re
    x_re = jnp.einsum('gnc,btgc->btgn', bb_re, u)
    x_im = jnp.einsum('gnc,btgc->btgn', bb_im, u)
    x_re = x_re.at[:, 0].add(a_re * h0_re - a_im * h0_im)
    x_im = x_im.at[:, 0].add(a_re * h0_im + a_im * h0_re)
    A_re = jnp.broadcast_to(a_re, x_re.shape)
    A_im = jnp.broadcast_to(a_im, x_im.shape)
    _, _, s_re, s_im = lax.associative_scan(_complex_affine_combine, (A_re, A_im, x_re, x_im), axis=1)
    y = (jnp.einsum('gcn,btgn->btgc', c_re, s_re) - jnp.einsum('gcn,btgn->btgc', c_im, s_im)
         + d * u)
    return y, s_re[:, -1], s_im[:, -1]


def s5_glu(y, w_a, w_b):
    B, T = y.shape[:2]
    z = jax.nn.gelu(y.reshape(B, T, SSM_W))
    return (z @ w_a) * jax.nn.sigmoid(z @ w_b)


def mem_kv(mem, mem_g, w_ck, w_cv, ck_g):
    m = rmsnorm(mem, mem_g)
    B, M, _ = m.shape
    mk = rmsnorm((m @ w_ck).reshape(B, M, X_HEADS, X_HEAD_DIM), ck_g)
    mv = (m @ w_cv).reshape(B, M, X_HEADS, X_HEAD_DIM)
    return mk, mv


def cross_attn(h, mk, mv, norm_g, w_cq, cq_g, w_co):
    B, T, _ = h.shape
    u = rmsnorm(h, norm_g)
    q = rmsnorm((u @ w_cq).reshape(B, T, X_HEADS, X_HEAD_DIM), cq_g)
    s = jnp.einsum('bqhd,bkhd->bhqk', q, mk, preferred_element_type=jnp.float32) * (X_HEAD_DIM ** -0.5)
    p = jax.nn.softmax(s, axis=-1).astype(mv.dtype)
    o = jnp.einsum('bhqk,bkhd->bqhd', p, mv).reshape(B, T, X_W)
    return h + o @ w_co


def setup_inputs(seed: int = 0) -> dict:
    key = jax.random.key(seed)
    ks = iter(jax.random.split(key, 64))
    f32 = jnp.float32
    L, D = DEPTH, D_MODEL
    n_pages = PAST_LEN // PAGE_SIZE
    n_pool = (DEC_BATCH * n_pages * 5) // 4

    def nrm(shape, scale):
        return jax.random.normal(next(ks), shape, f32) * scale

    def gain(shape):
        return 1.0 + nrm(shape, 0.02)

    inp = {}
    inp["x_prompt"] = nrm((BATCH, SEQ, D), 1.0)
    inp["x_sample"] = nrm((DEC_BATCH, DEC_SEQ, D), 1.0)
    inp["cache_k"] = nrm((L, n_pool, PAGE_SIZE, N_HEADS, HEAD_DIM), 1.0)
    inp["cache_v"] = nrm((L, n_pool, PAGE_SIZE, N_HEADS, HEAD_DIM), 1.0)
    inp["cache_logf"] = jax.nn.log_sigmoid(CACHE_FORGET_LOGIT + nrm((L, n_pool, PAGE_SIZE, N_HEADS), 0.5))
    inp["state_ssm_re"] = nrm((L, DEC_BATCH, N_GROUPS, STATE_N), 0.1)
    inp["state_ssm_im"] = nrm((L, DEC_BATCH, N_GROUPS, STATE_N), 0.1)
    inp["cache_mem_k"] = nrm((L, DEC_BATCH, MEM_TOKENS, X_HEADS, X_HEAD_DIM), 1.0)
    inp["cache_mem_v"] = nrm((L, DEC_BATCH, MEM_TOKENS, X_HEADS, X_HEAD_DIM), 1.0)
    perm = jax.random.permutation(next(ks), n_pool)[: DEC_BATCH * n_pages]
    inp["page_table"] = perm.reshape(DEC_BATCH, n_pages).astype(jnp.int32)
    inp["mem_prompt"] = nrm((BATCH, MEM_TOKENS, D), 1.0)
    inp["ffn1_norm"] = gain((L, D))
    inp["ffn1_w_gate"] = nrm((L, D, D_FF), D ** -0.5)
    inp["ffn1_w_up"] = nrm((L, D, D_FF), D ** -0.5)
    inp["ffn1_w_down"] = nrm((L, D_FF, D), D_FF ** -0.5)
    inp["mix_norm"] = gain((L, D))
    inp["w_in"] = nrm((L, D, IN_W), D ** -0.5)
    inp["b_forget"] = FORGET_BIAS_INIT + nrm((L, N_HEADS), 0.5)
    inp["q_norm"] = gain((L, HEAD_DIM))
    inp["k_norm"] = gain((L, HEAD_DIM))
    inp["ssm_lambda_re"] = -0.5 * jnp.exp(nrm((L, N_GROUPS, STATE_N), 0.02))
    inp["ssm_lambda_im"] = math.pi * jnp.arange(STATE_N, dtype=f32) + nrm((L, N_GROUPS, STATE_N), 0.01)
    inp["ssm_log_dt"] = jax.random.uniform(next(ks), (L, N_GROUPS), f32, math.log(1e-3), math.log(1e-1))
    b_scale = (2.0 * GROUP_CH) ** -0.5
    inp["ssm_b_re"] = nrm((L, N_GROUPS, STATE_N, GROUP_CH), b_scale)
    inp["ssm_b_im"] = nrm((L, N_GROUPS, STATE_N, GROUP_CH), b_scale)
    c_scale = (2.0 * STATE_N) ** -0.5
    inp["ssm_c_re"] = nrm((L, N_GROUPS, GROUP_CH, STATE_N), c_scale)
    inp["ssm_c_im"] = nrm((L, N_GROUPS, GROUP_CH, STATE_N), c_scale)
    inp["ssm_d"] = nrm((L, N_GROUPS, GROUP_CH), 1.0)
    inp["ssm_glu_w"] = nrm((L, SSM_W, D), SSM_W ** -0.5)
    inp["ssm_glu_v"] = nrm((L, SSM_W, D), SSM_W ** -0.5)
    inp["w_att_proj"] = nrm((L, ATT_W, D), ATT_W ** -0.5)
    inp["w_out"] = nrm((L, D, D), D ** -0.5)
    inp["cross_norm"] = gain((L, D))
    inp["mem_norm"] = gain((L, D))
    inp["w_cq"] = nrm((L, D, X_W), D ** -0.5)
    inp["w_ck"] = nrm((L, D, X_W), D ** -0.5)
    inp["w_cv"] = nrm((L, D, X_W), D ** -0.5)
    inp["cq_norm"] = gain((L, X_HEAD_DIM))
    inp["ck_norm"] = gain((L, X_HEAD_DIM))
    inp["w_co"] = nrm((L, X_W, D), X_W ** -0.5)
    inp["ffn2_norm"] = gain((L, D))
    inp["ffn2_w_gate"] = nrm((L, D, D_FF), D ** -0.5)
    inp["ffn2_w_up"] = nrm((L, D, D_FF), D ** -0.5)
    inp["ffn2_w_down"] = nrm((L, D_FF, D), D_FF ** -0.5)
    return inp


def reference(x_prompt, x_sample, cache_k, cache_v, cache_logf, state_ssm_re, state_ssm_im,
              cache_mem_k, cache_mem_v, page_table, mem_prompt,
              ffn1_norm, ffn1_w_gate, ffn1_w_up, ffn1_w_down,
              mix_norm, w_in, b_forget, q_norm, k_norm,
              ssm_lambda_re, ssm_lambda_im, ssm_log_dt, ssm_b_re, ssm_b_im, ssm_c_re, ssm_c_im, ssm_d,
              ssm_glu_w, ssm_glu_v, w_att_proj, w_out,
              cross_norm, mem_norm, w_cq, w_ck, w_cv, cq_norm, ck_norm, w_co,
              ffn2_norm, ffn2_w_gate, ffn2_w_up, ffn2_w_down):
    Bp = x_prompt.shape[0]
    Bd = x_sample.shape[0]
    past = page_table.shape[1] * PAGE_SIZE
    y_p, y_s = x_prompt, x_sample
    kp_l, vp_l, fp_l, hrp_l, hip_l, mkp_l, mvp_l = [], [], [], [], [], [], []
    ks_l, vs_l, fs_l, hrs_l, his_l = [], [], [], [], []

    for l in range(DEPTH):
        def trunk_layer(x, attend, h0_re, h0_im, mk, mv, l=l):
            h = swiglu_half(x, ffn1_norm[l], ffn1_w_gate[l], ffn1_w_up[l], ffn1_w_down[l])
            u = rmsnorm(h, mix_norm[l])
            q, k, v, logf, s, ga, gb = mixer_inputs(u, w_in[l], b_forget[l], q_norm[l], k_norm[l])
            o_att = attend(q, k, v, logf) @ w_att_proj[l]
            y_ssm, hr, hi = s5_ssm(s, h0_re, h0_im, ssm_lambda_re[l], ssm_lambda_im[l], ssm_log_dt[l],
                                   ssm_b_re[l], ssm_b_im[l], ssm_c_re[l], ssm_c_im[l], ssm_d[l])
            o_ssm = s5_glu(y_ssm, ssm_glu_w[l], ssm_glu_v[l])
            h = h + (ga * o_att + gb * o_ssm) @ w_out[l]
            h = cross_attn(h, mk, mv, cross_norm[l], w_cq[l], cq_norm[l], w_co[l])
            out = swiglu_half(h, ffn2_norm[l], ffn2_w_gate[l], ffn2_w_up[l], ffn2_w_down[l])
            return out, k, v, logf, hr, hi

        mk_p, mv_p = mem_kv(mem_prompt, mem_norm[l], w_ck[l], w_cv[l], ck_norm[l])
        z0 = jnp.zeros((Bp, N_GROUPS, STATE_N), x_prompt.dtype)
        y_p, k_p, v_p, f_p, hr_p, hi_p = trunk_layer(y_p, fox_prompt, z0, z0, mk_p, mv_p)
        kp_l.append(k_p); vp_l.append(v_p); fp_l.append(f_p)
        hrp_l.append(hr_p); hip_l.append(hi_p); mkp_l.append(mk_p); mvp_l.append(mv_p)

        k_past = cache_k[l][page_table].reshape(Bd, past, N_HEADS, HEAD_DIM)
        v_past = cache_v[l][page_table].reshape(Bd, past, N_HEADS, HEAD_DIM)
        f_past = cache_logf[l][page_table].reshape(Bd, past, N_HEADS)

        def attend_sample(q, k, v, logf, k_past=k_past, v_past=v_past, f_past=f_past):
            return fox_sample(q, k, v, logf, k_past, v_past, f_past)

        y_s, k_s, v_s, f_s, hr_s, hi_s = trunk_layer(y_s, attend_sample, state_ssm_re[l], state_ssm_im[l],
                                                     cache_mem_k[l], cache_mem_v[l])
        ks_l.append(k_s); vs_l.append(v_s); fs_l.append(f_s); hrs_l.append(hr_s); his_l.append(hi_s)

    return (y_p, y_s,
            jnp.stack(kp_l), jnp.stack(vp_l), jnp.stack(fp_l),
            jnp.stack(hrp_l), jnp.stack(hip_l), jnp.stack(mkp_l), jnp.stack(mvp_l),
            jnp.stack(ks_l), jnp.stack(vs_l), jnp.stack(fs_l),
            jnp.stack(hrs_l), jnp.stack(his_l))
```

```python
import functools
import math

import jax
import jax.numpy as jnp
import numpy as np
from jax import lax
from jax.experimental import pallas as pl
from jax.experimental.pallas import tpu as pltpu

F32 = jnp.float32
BF16 = jnp.bfloat16
EPS = 1e-6
LANES = 128
HEAD_DIM = 128
GROUP_CH = 16
STATE_N = 64
PAGE_SIZE = 128
X_HEAD_DIM = 128
NEG_INF = float("-inf")
M_INIT = -1e30
VMEM_LIMIT = 56 * 1024 * 1024
HIGHEST = lax.Precision.HIGHEST


def _cparams(*sem):
    return pltpu.CompilerParams(dimension_semantics=sem, vmem_limit_bytes=VMEM_LIMIT)


def _resident(shape):
    nd = len(shape)
    return pl.BlockSpec(shape, lambda *_: (0,) * nd, pipeline_mode=pl.Buffered(1))


def _row_tile(m, target, mult=16):
    best = None
    for t in range(mult, min(m, target) + 1, mult):
        if m % t == 0:
            best = t
    assert best is not None, (m, target)
    return best


def _rms(x, g):
    return (x * lax.rsqrt(jnp.mean(x * x, axis=-1, keepdims=True) + EPS)) * g


def _dot(a, b):
    return jnp.dot(a, b, preferred_element_type=F32)


def _dot_nt(a, b):
    return lax.dot_general(a, b, (((1,), (1,)), ((), ())), preferred_element_type=F32)


def _dot_exact(a, b):
    return jnp.dot(a, b, preferred_element_type=F32, precision=HIGHEST)


def _log_sigmoid(x):
    return -(jnp.maximum(-x, 0.0) + jnp.log1p(jnp.exp(-jnp.abs(x))))


def _gelu_tanh(x):
    return 0.5 * x * (1.0 + jnp.tanh(math.sqrt(2.0 / math.pi) * (x + 0.044715 * (x * x * x))))


def _ffn_kernel(x_ref, g_ref, wg_ref, wu_ref, wd_ref, *rest, n_ff, with_norm_out):
    if with_norm_out:
        g2_ref, o_ref, u_ref, h_ref = rest
    else:
        o_ref, h_ref = rest
    j = pl.program_id(1)

    @pl.when(j == 0)
    def _():
        h_ref[...] = _rms(x_ref[...], g_ref[...]).astype(BF16)

    h = h_ref[...]
    gate = _dot(h, wg_ref[...])
    up = _dot(h, wu_ref[...])
    act = (gate * jax.nn.sigmoid(gate) * up).astype(BF16)
    part = _dot(act, wd_ref[...])

    @pl.when(j == 0)
    def _():
        o_ref[...] = part

    @pl.when(j > 0)
    def _():
        o_ref[...] += part

    @pl.when(j == n_ff - 1)
    def _():
        y = x_ref[...] + 0.5 * o_ref[...]
        o_ref[...] = y
        if with_norm_out:
            u_ref[...] = _rms(y, g2_ref[...]).astype(BF16)


def _ffn(x, g, wg, wu, wd, g_next=None):
    m, d = x.shape
    ff = wg.shape[1]
    tm = _row_tile(m, 704)
    tf = _row_tile(ff, 512, LANES)
    n_ff = ff // tf
    with_norm_out = g_next is not None
    in_specs = [
        pl.BlockSpec((tm, d), lambda i, j: (i, 0)),
        pl.BlockSpec((1, d), lambda i, j: (0, 0)),
        pl.BlockSpec((d, tf), lambda i, j: (0, j)),
        pl.BlockSpec((d, tf), lambda i, j: (0, j)),
        pl.BlockSpec((tf, d), lambda i, j: (j, 0)),
    ]
    args = [x, g.reshape(1, d), wg, wu, wd]
    out_shape = [jax.ShapeDtypeStruct((m, d), F32)]
    out_specs = [pl.BlockSpec((tm, d), lambda i, j: (i, 0))]
    if with_norm_out:
        in_specs.append(pl.BlockSpec((1, d), lambda i, j: (0, 0)))
        args.append(g_next.reshape(1, d))
        out_shape.append(jax.ShapeDtypeStruct((m, d), BF16))
        out_specs.append(pl.BlockSpec((tm, d), lambda i, j: (i, 0)))
    res = pl.pallas_call(
        functools.partial(_ffn_kernel, n_ff=n_ff, with_norm_out=with_norm_out),
        grid=(m // tm, n_ff),
        in_specs=in_specs,
        out_specs=out_specs,
        out_shape=out_shape,
        scratch_shapes=[pltpu.VMEM((tm, d), BF16)],
        compiler_params=_cparams("parallel", "arbitrary"),
        name="ffn",
    )(*args)
    return res if with_norm_out else res[0]


def _head_norm(r, g):
    cols = []
    for h in range(r.shape[1] // HEAD_DIM):
        blk = r[:, h * HEAD_DIM:(h + 1) * HEAD_DIM]
        cols.append(_rms(blk, g))
    return cols[0] if len(cols) == 1 else jnp.concatenate(cols, axis=1)


def _proj_kernel(u_ref, w_ref, aux_ref, *outs, mode, norm_in):
    if norm_in:
        gin_ref, outs = outs[0], outs[1:]
        u = _rms(u_ref[...], gin_ref[...]).astype(BF16)
    else:
        u = u_ref[...]
    r = _dot(u, w_ref[...])
    if mode == "headnorm":
        r = _head_norm(r, aux_ref[...])
    elif mode == "sigmoid":
        r = jax.nn.sigmoid(r)
    elif mode == "logsigmoid":
        r = _log_sigmoid(r + aux_ref[...])
    for o in outs:
        o[...] = r.astype(o.dtype)


def _proj(u, w, mode="plain", aux=None, out_dtypes=(F32,), gain_in=None, tm_target=704):
    m, kdim = u.shape
    n = w.shape[1]
    tm = _row_tile(m, tm_target)
    tn = _row_tile(n, 1024, LANES)
    if aux is None:
        aux = jnp.zeros((1, LANES), F32)
    in_specs = [
        pl.BlockSpec((tm, kdim), lambda j, i: (i, 0)),
        pl.BlockSpec((kdim, tn), lambda j, i: (0, j)),
        pl.BlockSpec(aux.shape, lambda j, i: (0, 0)),
    ]
    args = [u, w, aux]
    if gain_in is not None:
        in_specs.append(pl.BlockSpec((1, kdim), lambda j, i: (0, 0)))
        args.append(gain_in.reshape(1, kdim))
    res = pl.pallas_call(
        functools.partial(_proj_kernel, mode=mode, norm_in=gain_in is not None),
        grid=(n // tn, m // tm),
        in_specs=in_specs,
        out_specs=[pl.BlockSpec((tm, tn), lambda j, i: (i, j)) for _ in out_dtypes],
        out_shape=[jax.ShapeDtypeStruct((m, n), dt) for dt in out_dtypes],
        compiler_params=_cparams("parallel", "parallel"),
        name="proj_" + mode,
    )(*args)
    return res


def _cumsum_kernel(x_ref, c_ref, ct_ref, carry_ref):
    @pl.when(pl.program_id(1) == 0)
    def _():
        carry_ref[...] = jnp.zeros_like(carry_ref)

    x = x_ref[...]
    tc = x.shape[0]
    tri = (lax.broadcasted_iota(jnp.int32, (tc, tc), 1)
           <= lax.broadcasted_iota(jnp.int32, (tc, tc), 0)).astype(F32)
    c = _dot_exact(tri, x) + carry_ref[...]
    c_ref[...] = c
    ct_ref[...] = c.T[:8, :]
    carry_ref[...] = c[tc - 1:tc, :]


def _cumsum(logf, n_seq, seq_len):
    tc = _row_tile(seq_len, 512, LANES)
    nc = seq_len // tc
    return pl.pallas_call(
        _cumsum_kernel,
        grid=(n_seq, nc),
        in_specs=[pl.BlockSpec((tc, LANES), lambda b, c: (b * nc + c, 0))],
        out_specs=[pl.BlockSpec((tc, LANES), lambda b, c: (b * nc + c, 0)),
                   pl.BlockSpec((None, 8, tc), lambda b, c: (b, 0, c))],
        out_shape=[jax.ShapeDtypeStruct((n_seq * seq_len, LANES), F32),
                   jax.ShapeDtypeStruct((n_seq, 8, seq_len), F32)],
        scratch_shapes=[pltpu.VMEM((1, LANES), F32)],
        compiler_params=_cparams("parallel", "arbitrary"),
        name="logf_cumsum",
    )(logf)


def _fox_prompt_kernel(qi_ref, kj_ref, q_ref, k_ref, v_ref, c_ref, ct_ref, o_ref,
                       m_ref, l_ref, acc_ref, *, n_heads, scale):
    t = pl.program_id(1)
    qi = qi_ref[t]
    kj = kj_ref[t]
    tq = q_ref.shape[0]
    tk = k_ref.shape[0]

    @pl.when(kj == 0)
    def _():
        m_ref[...] = jnp.full(m_ref.shape, M_INIT, F32)
        l_ref[...] = jnp.zeros_like(l_ref)
        acc_ref[...] = jnp.zeros_like(acc_ref)

    def step(masked):
        cblk = c_ref[...]
        lane = lax.broadcasted_iota(jnp.int32, cblk.shape, 1)
        if masked:
            keep = (lax.broadcasted_iota(jnp.int32, (tq, tk), 1)
                    <= lax.broadcasted_iota(jnp.int32, (tq, tk), 0))
        for h in range(n_heads):
            sl = slice(h * HEAD_DIM, (h + 1) * HEAD_DIM)
            ci = jnp.sum(jnp.where(lane == h, cblk, 0.0), axis=1, keepdims=True)
            cj = ct_ref[h:h + 1, :]
            s = _dot_nt(q_ref[:, sl].astype(BF16), k_ref[:, sl]) * scale + (ci - cj)
            if masked:
                s = jnp.where(keep, s, NEG_INF)
            m_old = m_ref[h]
            m_new = jnp.maximum(m_old, jnp.max(s, axis=1, keepdims=True))
            alpha = jnp.exp(m_old - m_new)
            p = jnp.exp(s - m_new)
            l_ref[h] = alpha * l_ref[h] + jnp.sum(p, axis=1, keepdims=True)
            acc_ref[h] = alpha * acc_ref[h] + _dot(p.astype(BF16), v_ref[:, sl])
            m_ref[h] = m_new

    @pl.when(kj < qi)
    def _():
        step(False)

    @pl.when(kj == qi)
    def _():
        step(True)
        for h in range(n_heads):
            o_ref[:, h * HEAD_DIM:(h + 1) * HEAD_DIM] = acc_ref[h] / l_ref[h]


def _fox_prompt(q, kb, vb, c, ct, n_seq, seq_len, m_total):
    aw = q.shape[1]
    n_heads = aw // HEAD_DIM
    tq = _row_tile(seq_len, 512, LANES)
    nq = seq_len // tq
    pairs = [(i, j) for i in range(nq) for j in range(i + 1)]
    qi_tab = jnp.asarray([p[0] for p in pairs], jnp.int32)
    kj_tab = jnp.asarray([p[1] for p in pairs], jnp.int32)
    grid_spec = pltpu.PrefetchScalarGridSpec(
        num_scalar_prefetch=2,
        grid=(n_seq, len(pairs)),
        in_specs=[
            pl.BlockSpec((tq, aw), lambda b, t, qi, kj: (b * nq + qi[t], 0)),
            pl.BlockSpec((tq, aw), lambda b, t, qi, kj: (b * nq + kj[t], 0)),
            pl.BlockSpec((tq, aw), lambda b, t, qi, kj: (b * nq + kj[t], 0)),
            pl.BlockSpec((tq, LANES), lambda b, t, qi, kj: (b * nq + qi[t], 0)),
            pl.BlockSpec((None, 8, tq), lambda b, t, qi, kj: (b, 0, kj[t])),
        ],
        out_specs=pl.BlockSpec((tq, aw), lambda b, t, qi, kj: (b * nq + qi[t], 0)),
        scratch_shapes=[pltpu.VMEM((n_heads, tq, 1), F32),
                        pltpu.VMEM((n_heads, tq, 1), F32),
                        pltpu.VMEM((n_heads, tq, HEAD_DIM), F32)],
    )
    return pl.pallas_call(
        functools.partial(_fox_prompt_kernel, n_heads=n_heads, scale=HEAD_DIM ** -0.5),
        grid_spec=grid_spec,
        out_shape=jax.ShapeDtypeStruct((m_total, aw), F32),
        compiler_params=_cparams("parallel", "arbitrary"),
        name="fox_prompt",
    )(qi_tab, kj_tab, q, kb, vb, c, ct)


def _fox_sample_kernel(pt_ref, q_ref, kn_ref, vn_ref, lfn_ref, lfnt_ref, *rest,
                       n_heads, n_pg, scale):
    lft_refs = rest[:n_pg]
    k_refs = rest[n_pg:2 * n_pg]
    v_refs = rest[2 * n_pg:3 * n_pg]
    a_in_ref, o_ref, m_ref, l_ref, acc_ref, carry_ref = rest[3 * n_pg:]
    del pt_ref, a_in_ref
    s_idx = pl.program_id(1)
    n_steps = pl.num_programs(1)
    tn = q_ref.shape[0]

    @pl.when(s_idx == 0)
    def _():
        m_ref[...] = jnp.full(m_ref.shape, M_INIT, F32)
        l_ref[...] = jnp.zeros_like(l_ref)
        acc_ref[...] = jnp.zeros_like(acc_ref)
        carry_ref[...] = jnp.zeros_like(carry_ref)

    tri_n = (lax.broadcasted_iota(jnp.int32, (tn, tn), 1)
             <= lax.broadcasted_iota(jnp.int32, (tn, tn), 0)).astype(F32)
    cq = _dot_exact(tri_n, lfn_ref[...])

    later = (lax.broadcasted_iota(jnp.int32, (PAGE_SIZE, PAGE_SIZE), 0)
             > lax.broadcasted_iota(jnp.int32, (PAGE_SIZE, PAGE_SIZE), 1)).astype(F32)
    rev = [None] * n_pg
    carry = carry_ref[...]
    for r in range(n_pg - 1, -1, -1):
        lft = lft_refs[r][...]
        rev[r] = _dot_exact(lft, later) + carry
        carry = carry + jnp.sum(lft, axis=1, keepdims=True)
    carry_ref[...] = carry

    for h in range(n_heads):
        sl = slice(h * HEAD_DIM, (h + 1) * HEAD_DIM)
        qh = q_ref[:, sl].astype(BF16)
        cq_h = cq[:, h:h + 1]
        s_parts = []
        for r in range(n_pg):
            kh = k_refs[r][pl.ds(h, PAGE_SIZE, stride=n_heads), :].astype(BF16)
            s_parts.append(_dot_nt(qh, kh) * scale + (cq_h + rev[r][h:h + 1, :]))
        m_old = m_ref[h]
        m_new = m_old
        for sp in s_parts:
            m_new = jnp.maximum(m_new, jnp.max(sp, axis=1, keepdims=True))
        alpha = jnp.exp(m_old - m_new)
        l_new = alpha * l_ref[h]
        acc = alpha * acc_ref[h]
        for r in range(n_pg):
            p = jnp.exp(s_parts[r] - m_new)
            l_new = l_new + jnp.sum(p, axis=1, keepdims=True)
            vh = v_refs[r][pl.ds(h, PAGE_SIZE, stride=n_heads), :].astype(BF16)
            acc = acc + _dot(p.astype(BF16), vh)
        m_ref[h] = m_new
        l_ref[h] = l_new
        acc_ref[h] = acc

    @pl.when(s_idx == n_steps - 1)
    def _():
        upto = (lax.broadcasted_iota(jnp.int32, (LANES, LANES), 0)
                <= lax.broadcasted_iota(jnp.int32, (LANES, LANES), 1)).astype(F32)
        cq_t = _dot_exact(lfnt_ref[...], upto)
        keep = (lax.broadcasted_iota(jnp.int32, (tn, LANES), 1)
                <= lax.broadcasted_iota(jnp.int32, (tn, LANES), 0))
        pad = jnp.zeros((LANES - tn, HEAD_DIM), F32)
        for h in range(n_heads):
            sl = slice(h * HEAD_DIM, (h + 1) * HEAD_DIM)
            qh = q_ref[:, sl].astype(BF16)
            kh = jnp.concatenate([kn_ref[:, sl], pad], axis=0).astype(BF16)
            vh = jnp.concatenate([vn_ref[:, sl], pad], axis=0).astype(BF16)
            s = _dot_nt(qh, kh) * scale + (cq[:, h:h + 1] - cq_t[h:h + 1, :])
            s = jnp.where(keep, s, NEG_INF)
            m_old = m_ref[h]
            m_new = jnp.maximum(m_old, jnp.max(s, axis=1, keepdims=True))
            alpha = jnp.exp(m_old - m_new)
            p = jnp.exp(s - m_new)
            l_new = alpha * l_ref[h] + jnp.sum(p, axis=1, keepdims=True)
            acc = alpha * acc_ref[h] + _dot(p.astype(BF16), vh)
            o_ref[:, sl] = acc / l_new


def _fox_sample(a_buf, q, k, v, logf, logf_t, cache_k, cache_v, cache_lft, page_table,
                layer, n_seq, n_new, row0):
    aw = q.shape[1]
    n_heads = aw // HEAD_DIM
    n_pages = page_table.shape[1]
    n_pg = 8 if n_pages % 8 == 0 else (4 if n_pages % 4 == 0 else 1)
    n_steps = n_pages // n_pg
    blk0 = row0 // n_new

    def new_rows(b, s, pt):
        return (blk0 + b, 0)

    def page_map(r):
        def f(b, s, pt):
            return (layer, pt[b, (n_steps - 1 - s) * n_pg + r], 0, 0)
        return f

    in_specs = [
        pl.BlockSpec((n_new, aw), new_rows),
        pl.BlockSpec((n_new, aw), new_rows),
        pl.BlockSpec((n_new, aw), new_rows),
        pl.BlockSpec((n_new, LANES), new_rows),
        pl.BlockSpec((None, n_heads, LANES), lambda b, s, pt: (b, 0, 0)),
    ]
    in_specs += [pl.BlockSpec((None, None, n_heads, PAGE_SIZE), page_map(r)) for r in range(n_pg)]
    in_specs += [pl.BlockSpec((None, None, PAGE_SIZE * n_heads, HEAD_DIM), page_map(r))
                 for r in range(n_pg)]
    in_specs += [pl.BlockSpec((None, None, PAGE_SIZE * n_heads, HEAD_DIM), page_map(r))
                 for r in range(n_pg)]
    in_specs.append(pl.BlockSpec(memory_space=pl.ANY))
    n_in = len(in_specs)
    grid_spec = pltpu.PrefetchScalarGridSpec(
        num_scalar_prefetch=1,
        grid=(n_seq, n_steps),
        in_specs=in_specs,
        out_specs=pl.BlockSpec((n_new, aw), new_rows),
        scratch_shapes=[pltpu.VMEM((n_heads, n_new, 1), F32),
                        pltpu.VMEM((n_heads, n_new, 1), F32),
                        pltpu.VMEM((n_heads, n_new, HEAD_DIM), F32),
                        pltpu.VMEM((n_heads, 1), F32)],
    )
    args = ([page_table, q, k, v, logf, logf_t] + [cache_lft] * n_pg + [cache_k] * n_pg
            + [cache_v] * n_pg + [a_buf])
    return pl.pallas_call(
        functools.partial(_fox_sample_kernel, n_heads=n_heads, n_pg=n_pg, scale=HEAD_DIM ** -0.5),
        grid_spec=grid_spec,
        out_shape=jax.ShapeDtypeStruct(a_buf.shape, a_buf.dtype),
        input_output_aliases={n_in: 0},
        compiler_params=_cparams("parallel", "arbitrary"),
        name="fox_sample",
    )(*args)


def _cmul(a1, a2, x):
    return a1 * x + a2 * pltpu.roll(x, STATE_N, axis=1)


def _ssm_kernel(u_ref, ws_ref, wo_ref, tp_ref, dsk_ref, ap_ref, *rest, gb, n_chunks, has_h0):
    if has_h0:
        h0_ref, z_ref, hf_ref = rest
    else:
        z_ref, hf_ref = rest
    rows = u_ref.shape[1]
    n_streams = rows // n_chunks
    for g in range(gb):
        u = u_ref[g]
        ub = u.astype(BF16)
        x = _dot(ub, ws_ref[g])
        ap = ap_ref[g]
        if has_h0:
            hb = h0_ref[g]
            hfin = _cmul(ap[0:1, :], ap[1:2, :], hb) + x
        else:
            row = lax.broadcasted_iota(jnp.int32, x.shape, 0) % n_chunks
            hs = x
            sh = 1
            lvl = 0
            while sh < n_chunks:
                prev = jnp.where(row >= sh, pltpu.roll(hs, sh, axis=0), 0.0)
                hs = hs + _cmul(ap[2 * lvl:2 * lvl + 1, :], ap[2 * lvl + 1:2 * lvl + 2, :], prev)
                sh *= 2
                lvl += 1
            hfin = hs
            hb = jnp.where(row >= 1, pltpu.roll(hs, 1, axis=0), 0.0)
        y = _dot(ub, tp_ref[g]) + _dot(hb.astype(BF16), wo_ref[g]) + u * dsk_ref[g]
        z_ref[g] = _gelu_tanh(y)
        for s in range(n_streams):
            r = s * n_chunks + n_chunks - 1
            hf_ref[g, s:s + 1, :] = hfin[r:r + 1, :]


def _ssm(uf, prm, n_chunks, h0=None):
    n_groups, rows, lc = uf.shape
    n_streams = rows // n_chunks
    gb = 4 if n_groups % 4 == 0 else 1
    ws, wo, tp, dsk, ap = prm
    g3 = lambda i: (i, 0, 0)
    in_specs = [
        pl.BlockSpec((gb, rows, lc), g3),
        pl.BlockSpec((gb,) + ws.shape[1:], g3),
        pl.BlockSpec((gb,) + wo.shape[1:], g3),
        pl.BlockSpec((gb,) + tp.shape[1:], g3),
        pl.BlockSpec((gb,) + dsk.shape[1:], g3),
        pl.BlockSpec((gb,) + ap.shape[1:], g3),
    ]
    args = [uf, ws, wo, tp, dsk, ap]
    if h0 is not None:
        in_specs.append(pl.BlockSpec((gb, n_streams, 2 * STATE_N), g3))
        args.append(h0)
    return pl.pallas_call(
        functools.partial(_ssm_kernel, gb=gb, n_chunks=n_chunks, has_h0=h0 is not None),
        grid=(n_groups // gb,),
        in_specs=in_specs,
        out_specs=[pl.BlockSpec((gb, rows, lc), g3),
                   pl.BlockSpec((gb, n_streams, 2 * STATE_N), g3)],
        out_shape=[jax.ShapeDtypeStruct((n_groups, rows, lc), F32),
                   jax.ShapeDtypeStruct((n_groups, n_streams, 2 * STATE_N), F32)],
        compiler_params=_cparams("parallel"),
        name="s5_ssm",
    )(*args)


def _ssm_params(lam_re, lam_im, log_dt, b_re, b_im, c_re, c_im, d, chunk, n_chunks):
    hp = dict(precision=HIGHEST)
    dt = jnp.exp(log_dt)[:, None]
    mag = jnp.exp(lam_re * dt)
    a_re = mag * jnp.cos(lam_im * dt)
    a_im = mag * jnp.sin(lam_im * dt)
    den = lam_re * lam_re + lam_im * lam_im
    z_re = ((a_re - 1) * lam_re + a_im * lam_im) / den
    z_im = (a_im * lam_re - (a_re - 1) * lam_im) / den
    bb_re = z_re[..., None] * b_re - z_im[..., None] * b_im
    bb_im = z_re[..., None] * b_im + z_im[..., None] * b_re
    pr, pi = [jnp.ones_like(a_re)], [jnp.zeros_like(a_im)]
    for _ in range(chunk):
        pr.append(pr[-1] * a_re - pi[-1] * a_im)
        pi.append(pr[-2] * a_im + pi[-1] * a_re)
    pw_re = jnp.stack(pr)
    pw_im = jnp.stack(pi)
    n_groups = a_re.shape[0]
    dec_re = pw_re[chunk - 1::-1][:chunk]
    dec_im = pw_im[chunk - 1::-1][:chunk]
    ws_re = jnp.einsum('jgn,gnc->gjcn', dec_re, bb_re) - jnp.einsum('jgn,gnc->gjcn', dec_im, bb_im)
    ws_im = jnp.einsum('jgn,gnc->gjcn', dec_re, bb_im) + jnp.einsum('jgn,gnc->gjcn', dec_im, bb_re)
    ws = jnp.concatenate([ws_re, ws_im], axis=-1).reshape(n_groups, chunk * GROUP_CH, 2 * STATE_N)
    ca_re = c_re[None] * pw_re[1:, :, None, :] - c_im[None] * pw_im[1:, :, None, :]
    ca_im = c_re[None] * pw_im[1:, :, None, :] + c_im[None] * pw_re[1:, :, None, :]
    wo = jnp.concatenate([ca_re, -ca_im], axis=-1)
    wo = wo.transpose(1, 3, 0, 2).reshape(n_groups, 2 * STATE_N, chunk * GROUP_CH)
    k0_re = c_re[None] * pw_re[:chunk, :, None, :] - c_im[None] * pw_im[:chunk, :, None, :]
    k0_im = c_re[None] * pw_im[:chunk, :, None, :] + c_im[None] * pw_re[:chunk, :, None, :]
    km = (jnp.einsum('mgcn,gnd->gmcd', k0_re, bb_re, **hp)
          - jnp.einsum('mgcn,gnd->gmcd', k0_im, bb_im, **hp))
    lag = np.arange(chunk)[None, :] - np.arange(chunk)[:, None]
    tp = km[:, np.clip(lag, 0, chunk - 1)]
    tp = jnp.where((lag >= 0)[None, :, :, None, None], tp, 0.0)
    tp = tp.transpose(0, 1, 4, 2, 3).reshape(n_groups, chunk * GROUP_CH, chunk * GROUP_CH)
    dsk = jnp.tile(d[:, None, :], (1, chunk, 1)).reshape(n_groups, 1, chunk * GROUP_CH)
    qr, qi = pw_re[chunk], pw_im[chunk]
    rows = []
    sh = 1
    while True:
        rows.append(jnp.concatenate([qr, qr], axis=-1))
        rows.append(jnp.concatenate([-qi, qi], axis=-1))
        sh *= 2
        if sh >= n_chunks:
            break
        qr, qi = qr * qr - qi * qi, 2 * qr * qi
    ap = jnp.stack(rows, axis=1)
    pad = (-ap.shape[1]) % 8
    if pad:
        ap = jnp.concatenate([ap, jnp.zeros((n_groups, pad, 2 * STATE_N), F32)], axis=1)
    return ws.astype(BF16), wo.astype(BF16), tp.astype(BF16), dsk, ap


def _mix_kernel(h_ref, a_ref, z_ref, ga_ref, gb_ref, wp_ref, wa_ref, wb_ref, wo_ref, o_ref):
    o_att = _dot(a_ref[...].astype(BF16), wp_ref[...])
    zb = z_ref[...].astype(BF16)
    o_ssm = _dot(zb, wa_ref[...]) * jax.nn.sigmoid(_dot(zb, wb_ref[...]))
    g = (ga_ref[...] * o_att + gb_ref[...] * o_ssm).astype(BF16)
    o_ref[...] = h_ref[...] + _dot(g, wo_ref[...])


def _mix(h, a, z, ga, gb, wp, wa, wb, wo):
    m, d = h.shape
    tm = _row_tile(m, 256)
    row = lambda w: pl.BlockSpec((tm, w), lambda i: (i, 0))
    return pl.pallas_call(
        _mix_kernel,
        grid=(m // tm,),
        in_specs=[row(d), row(a.shape[1]), row(z.shape[1]), row(d), row(d),
                  _resident(wp.shape), _resident(wa.shape), _resident(wb.shape), _resident(wo.shape)],
        out_specs=row(d),
        out_shape=jax.ShapeDtypeStruct((m, d), F32),
        compiler_params=_cparams("parallel"),
        name="mix",
    )(h, a, z, ga, gb, wp, wa, wb, wo)


def _xattn_kernel(h_ref, g_ref, wq_ref, qg_ref, mk_ref, mv_ref, wo_ref, o_ref, *, n_heads, scale):
    x = h_ref[...]
    u = _rms(x, g_ref[...]).astype(BF16)
    qx = _dot(u, wq_ref[...])
    outs = []
    for h in range(n_heads):
        sl = slice(h * X_HEAD_DIM, (h + 1) * X_HEAD_DIM)
        qh = _rms(qx[:, sl], qg_ref[...]).astype(BF16)
        s = _dot_nt(qh, mk_ref[:, sl].astype(BF16)) * scale
        e = jnp.exp(s - jnp.max(s, axis=1, keepdims=True))
        p = e / jnp.sum(e, axis=1, keepdims=True)
        outs.append(_dot(p.astype(BF16), mv_ref[:, sl].astype(BF16)))
    o = jnp.concatenate(outs, axis=1).astype(BF16)
    o_ref[...] = x + _dot(o, wo_ref[...])


def _xattn(h, g, wq, qg, mk, mv, wo, n_seq, seq_len, row0, tm_target):
    m, d = h.shape
    xw = wq.shape[1]
    n_mem = mk.shape[0] // n_seq
    tm = _row_tile(seq_len, tm_target, 8)
    nt = seq_len // tm
    blk0 = row0 // tm
    rows = lambda b, i: (blk0 + b * nt + i, 0)
    return pl.pallas_call(
        functools.partial(_xattn_kernel, n_heads=xw // X_HEAD_DIM, scale=X_HEAD_DIM ** -0.5),
        grid=(n_seq, nt),
        in_specs=[pl.BlockSpec((tm, d), rows),
                  _resident((1, d)), _resident(wq.shape), _resident((1, X_HEAD_DIM)),
                  pl.BlockSpec((n_mem, xw), lambda b, i: (b, 0)),
                  pl.BlockSpec((n_mem, xw), lambda b, i: (b, 0)),
                  _resident(wo.shape)],
        out_specs=pl.BlockSpec((tm, d), rows),
        out_shape=jax.ShapeDtypeStruct((m, d), F32),
        input_output_aliases={0: 0},
        compiler_params=_cparams("parallel", "arbitrary"),
        name="xattn",
    )(h, g.reshape(1, d), wq, qg.reshape(1, X_HEAD_DIM), mk, mv, wo)


def kernel(x_prompt, x_sample, cache_k, cache_v, cache_logf, state_ssm_re, state_ssm_im, cache_mem_k, cache_mem_v, page_table, mem_prompt, ffn1_norm, ffn1_w_gate, ffn1_w_up, ffn1_w_down, mix_norm, w_in, b_forget, q_norm, k_norm, ssm_lambda_re, ssm_lambda_im, ssm_log_dt, ssm_b_re, ssm_b_im, ssm_c_re, ssm_c_im, ssm_d, ssm_glu_w, ssm_glu_v, w_att_proj, w_out, cross_norm, mem_norm, w_cq, w_ck, w_cv, cq_norm, ck_norm, w_co, ffn2_norm, ffn2_w_gate, ffn2_w_up, ffn2_w_down):
    bp, t_p, d = x_prompt.shape
    bd, t_s, _ = x_sample.shape
    depth = ffn1_norm.shape[0]
    aw = w_att_proj.shape[1]
    n_heads = aw // HEAD_DIM
    sw = ssm_glu_w.shape[1]
    n_groups = sw // GROUP_CH
    xw = w_cq.shape[2]
    x_heads = xw // X_HEAD_DIM
    n_mem = mem_prompt.shape[1]
    n_pool = cache_k.shape[1]
    mp, ms = bp * t_p, bd * t_s
    mt = mp + ms
    chunk_p = 16
    nc_p = t_p // chunk_p
    assert n_heads <= 8 and t_s == 8 and t_p % chunk_p == 0

    x = jnp.concatenate([x_prompt.reshape(mp, d), x_sample.reshape(ms, d)], axis=0)
    mem = mem_prompt.reshape(bp * n_mem, d)
    ck = cache_k.reshape(depth, n_pool, PAGE_SIZE * n_heads, HEAD_DIM)
    cv = cache_v.reshape(depth, n_pool, PAGE_SIZE * n_heads, HEAD_DIM)
    clft = cache_logf.transpose(0, 1, 3, 2)

    o_q, o_k, o_v = 0, aw, 2 * aw
    o_f = 3 * aw
    o_s = o_f + n_heads
    o_ga = o_s + sw
    o_gb = o_ga + d

    outs = {k: [] for k in ("kp", "vp", "fp", "hrp", "hip", "mkp", "mvp", "ks", "vs", "fs", "hrs", "his")}
    for l in range(depth):
        bf = lambda w: w.astype(BF16)
        wl = w_in[l]
        w_f = jnp.pad(wl[:, o_f:o_s], ((0, 0), (0, LANES - n_heads)))
        b_f = jnp.pad(b_forget[l], (0, LANES - n_heads)).reshape(1, LANES)

        h1, u = _ffn(x, ffn1_norm[l], bf(ffn1_w_gate[l]), bf(ffn1_w_up[l]), bf(ffn1_w_down[l]),
                     g_next=mix_norm[l])
        (q,) = _proj(u, bf(wl[:, o_q:o_k]), "headnorm", q_norm[l].reshape(1, HEAD_DIM))
        k, kb = _proj(u, bf(wl[:, o_k:o_v]), "headnorm", k_norm[l].reshape(1, HEAD_DIM), (F32, BF16))
        v, vb = _proj(u, bf(wl[:, o_v:o_f]), "plain", None, (F32, BF16))
        (lf,) = _proj(u, bf(w_f), "logsigmoid", b_f)
        (s,) = _proj(u, bf(wl[:, o_s:o_ga]))
        (ga,) = _proj(u, bf(wl[:, o_ga:o_gb]), "sigmoid")
        (gb,) = _proj(u, bf(wl[:, o_gb:]), "sigmoid")

        c, ct = _cumsum(lf, bp, t_p)
        att = _fox_prompt(q, kb, vb, c, ct, bp, t_p, mt)
        lf_s = lf[mp:].reshape(bd, t_s, LANES)
        lf_st = lf_s[:, :, :n_heads].transpose(0, 2, 1)
        lf_st = jnp.pad(lf_st, ((0, 0), (0, 0), (0, LANES - t_s)))
        att = _fox_sample(att, q, k, v, lf, lf_st, ck, cv, clft, page_table, l, bd, t_s, mp)

        prm_p = _ssm_params(ssm_lambda_re[l], ssm_lambda_im[l], ssm_log_dt[l], ssm_b_re[l], ssm_b_im[l],
                            ssm_c_re[l], ssm_c_im[l], ssm_d[l], chunk_p, nc_p)
        prm_s = _ssm_params(ssm_lambda_re[l], ssm_lambda_im[l], ssm_log_dt[l], ssm_b_re[l], ssm_b_im[l],
                            ssm_c_re[l], ssm_c_im[l], ssm_d[l], t_s, 1)
        uf_p = (s[:mp].reshape(bp, nc_p, chunk_p, n_groups, GROUP_CH).transpose(3, 0, 1, 2, 4)
                .reshape(n_groups, bp * nc_p, chunk_p * GROUP_CH))
        uf_s = (s[mp:].reshape(bd, t_s, n_groups, GROUP_CH).transpose(2, 0, 1, 3)
                .reshape(n_groups, bd, t_s * GROUP_CH))
        h0 = jnp.concatenate([state_ssm_re[l], state_ssm_im[l]], axis=-1).transpose(1, 0, 2)
        zf_p, hf_p = _ssm(uf_p, prm_p, nc_p)
        zf_s, hf_s = _ssm(uf_s, prm_s, 1, h0)
        z_p = (zf_p.reshape(n_groups, bp, nc_p, chunk_p, GROUP_CH).transpose(1, 2, 3, 0, 4)
               .reshape(mp, sw))
        z_s = zf_s.reshape(n_groups, bd, t_s, GROUP_CH).transpose(1, 2, 0, 3).reshape(ms, sw)
        z = jnp.concatenate([z_p, z_s], axis=0)

        h2 = _mix(h1, att, z, ga, gb, bf(w_att_proj[l]), bf(ssm_glu_w[l]), bf(ssm_glu_v[l]), bf(w_out[l]))

        mk_p, = _proj(mem, bf(w_ck[l]), "headnorm", ck_norm[l].reshape(1, X_HEAD_DIM),
                      gain_in=mem_norm[l], tm_target=512)
        mv_p, = _proj(mem, bf(w_cv[l]), "plain", None, gain_in=mem_norm[l], tm_target=512)
        wq_x, wo_x = bf(w_cq[l]), bf(w_co[l])
        h3 = _xattn(h2, cross_norm[l], wq_x, cq_norm[l], mk_p, mv_p, wo_x, bp, t_p, 0, 512)
        h3 = _xattn(h3, cross_norm[l], wq_x, cq_norm[l],
                    cache_mem_k[l].reshape(bd * n_mem, xw), cache_mem_v[l].reshape(bd * n_mem, xw),
                    wo_x, bd, t_s, mp, t_s)

        x = _ffn(h3, ffn2_norm[l], bf(ffn2_w_gate[l]), bf(ffn2_w_up[l]), bf(ffn2_w_down[l]))

        outs["kp"].append(k[:mp].reshape(bp, t_p, n_heads, HEAD_DIM))
        outs["vp"].append(v[:mp].reshape(bp, t_p, n_heads, HEAD_DIM))
        outs["fp"].append(lf[:mp, :n_heads].reshape(bp, t_p, n_heads))
        outs["hrp"].append(hf_p[:, :, :STATE_N].transpose(1, 0, 2))
        outs["hip"].append(hf_p[:, :, STATE_N:].transpose(1, 0, 2))
        outs["mkp"].append(mk_p.reshape(bp, n_mem, x_heads, X_HEAD_DIM))
        outs["mvp"].append(mv_p.reshape(bp, n_mem, x_heads, X_HEAD_DIM))
        outs["ks"].append(k[mp:].reshape(bd, t_s, n_heads, HEAD_DIM))
        outs["vs"].append(v[mp:].reshape(bd, t_s, n_heads, HEAD_DIM))
        outs["fs"].append(lf[mp:, :n_heads].reshape(bd, t_s, n_heads))
        outs["hrs"].append(hf_s[:, :, :STATE_N].transpose(1, 0, 2))
        outs["his"].append(hf_s[:, :, STATE_N:].transpose(1, 0, 2))

    st = lambda name: jnp.stack(outs[name])
    return (x[:mp].reshape(bp, t_p, d), x[mp:].reshape(bd, t_s, d),
            st("kp"), st("vp"), st("fp"), st("hrp"), st("hip"), st("mkp"), st("mvp"),
            st("ks"), st("vs"), st("fs"), st("hrs"), st("his"))
```

```python
import functools
import math

import jax
import jax.numpy as jnp
import numpy as np
from jax import lax
from jax.experimental import pallas as pl
from jax.experimental.pallas import tpu as pltpu

F32 = jnp.float32
BF16 = jnp.bfloat16
EPS = 1e-6
LANES = 128
HEAD_DIM = 128
GROUP_CH = 16
STATE_N = 64
PAGE_SIZE = 128
X_HEAD_DIM = 128
SLAB_GROUPS = LANES // GROUP_CH
NEG_INF = float("-inf")
M_INIT = -1e30
LOG2E = math.log2(math.e)
VMEM_LIMIT = 56 * 1024 * 1024
HIGHEST = lax.Precision.HIGHEST


def _cparams(*sem):
    return pltpu.CompilerParams(dimension_semantics=sem, vmem_limit_bytes=VMEM_LIMIT)


def _layer_resident(w, layer):
    return pl.BlockSpec((None,) + w.shape[1:], lambda *_: (layer, 0, 0), pipeline_mode=pl.Buffered(1))


def _row_tile(m, target, mult=16):
    best = None
    for t in range(mult, min(m, target) + 1, mult):
        if m % t == 0:
            best = t
    assert best is not None, (m, target)
    return best


def _rms(x, g):
    return (x * lax.rsqrt(jnp.mean(x * x, axis=-1, keepdims=True) + EPS)) * g


def _dot(a, b):
    return jnp.dot(a, b, preferred_element_type=F32)


def _dot_nt(a, b):
    return lax.dot_general(a, b, (((1,), (1,)), ((), ())), preferred_element_type=F32)


def _dot_exact(a, b):
    return jnp.dot(a, b, preferred_element_type=F32, precision=HIGHEST)


def _log_sigmoid(x):
    return -(jnp.maximum(-x, 0.0) + jnp.log1p(jnp.exp(-jnp.abs(x))))


def _gelu_tanh(x):
    return 0.5 * x * (1.0 + jnp.tanh(math.sqrt(2.0 / math.pi) * (x + 0.044715 * (x * x * x))))


def _ffn_kernel(x_ref, g_ref, wg_ref, wu_ref, wd_ref, *rest, n_ff, with_norm_out):
    if with_norm_out:
        g2_ref, o_ref, u_ref, h_ref = rest
    else:
        o_ref, h_ref = rest
    j = pl.program_id(1)

    @pl.when(j == 0)
    def _():
        h_ref[...] = _rms(x_ref[...], g_ref[...]).astype(BF16)

    h = h_ref[...]
    gate = _dot(h, wg_ref[...])
    up = _dot(h, wu_ref[...])
    act = (gate * jax.nn.sigmoid(gate) * up).astype(BF16)
    part = _dot(act, wd_ref[...])

    @pl.when(j == 0)
    def _():
        o_ref[...] = part

    @pl.when(j > 0)
    def _():
        o_ref[...] += part

    @pl.when(j == n_ff - 1)
    def _():
        y = x_ref[...] + 0.5 * o_ref[...]
        o_ref[...] = y
        if with_norm_out:
            u_ref[...] = _rms(y, g2_ref[...]).astype(BF16)


def _ffn(x, g, wg, wu, wd, layer, g_next=None):
    m, d = x.shape
    ff = wg.shape[2]
    tm = _row_tile(m, 704)
    tf = _row_tile(ff, 512, LANES)
    n_ff = ff // tf
    with_norm_out = g_next is not None
    in_specs = [
        pl.BlockSpec((tm, d), lambda i, j: (i, 0)),
        pl.BlockSpec((1, d), lambda i, j: (0, 0)),
        pl.BlockSpec((None, d, tf), lambda i, j: (layer, 0, j)),
        pl.BlockSpec((None, d, tf), lambda i, j: (layer, 0, j)),
        pl.BlockSpec((None, tf, d), lambda i, j: (layer, j, 0)),
    ]
    args = [x, g.reshape(1, d), wg, wu, wd]
    out_shape = [jax.ShapeDtypeStruct((m, d), F32)]
    out_specs = [pl.BlockSpec((tm, d), lambda i, j: (i, 0))]
    if with_norm_out:
        in_specs.append(pl.BlockSpec((1, d), lambda i, j: (0, 0)))
        args.append(g_next.reshape(1, d))
        out_shape.append(jax.ShapeDtypeStruct((m, d), BF16))
        out_specs.append(pl.BlockSpec((tm, d), lambda i, j: (i, 0)))
    res = pl.pallas_call(
        functools.partial(_ffn_kernel, n_ff=n_ff, with_norm_out=with_norm_out),
        grid=(m // tm, n_ff),
        in_specs=in_specs,
        out_specs=out_specs,
        out_shape=out_shape,
        scratch_shapes=[pltpu.VMEM((tm, d), BF16)],
        compiler_params=_cparams("parallel", "arbitrary"),
        name="ffn",
    )(*args)
    return res if with_norm_out else res[0]


def _head_norm(r, g):
    cols = []
    for h in range(r.shape[1] // HEAD_DIM):
        cols.append(_rms(r[:, h * HEAD_DIM:(h + 1) * HEAD_DIM], g))
    return cols[0] if len(cols) == 1 else jnp.concatenate(cols, axis=1)


def _proj_kernel(u_ref, w_ref, aux_ref, *rest, mode, norm_in, n_out):
    if norm_in:
        gin_ref, rest = rest[0], rest[1:]
        u = _rms(u_ref[...], gin_ref[...]).astype(BF16)
    else:
        u = u_ref[...]
    outs = rest[len(rest) - n_out:]
    r = _dot(u, w_ref[...])
    if mode == "headnorm":
        r = _head_norm(r, aux_ref[...])
    elif mode == "sigmoid":
        r = jax.nn.sigmoid(r)
    elif mode == "logsigmoid":
        r = _log_sigmoid(r + aux_ref[...])
    for o in outs:
        o[...] = r.astype(o.dtype)


def _proj(u, w, layer, col0, n, row0, n_rows, tm, mode="plain", aux=None, outs=((F32, None),),
          gain_in=None):
    kdim = u.shape[1]
    tn = _row_tile(math.gcd(n, col0), 1024, LANES)
    assert row0 % tm == 0 and n_rows % tm == 0
    rb0, cb0 = row0 // tm, col0 // tn
    if aux is None:
        aux = jnp.zeros((1, LANES), F32)
    in_specs = [
        pl.BlockSpec((tm, kdim), lambda j, i: (rb0 + i, 0)),
        pl.BlockSpec((None, kdim, tn), lambda j, i: (layer, 0, cb0 + j)),
        pl.BlockSpec(aux.shape, lambda j, i: (0, 0)),
    ]
    args = [u, w, aux]
    if gain_in is not None:
        in_specs.append(pl.BlockSpec((1, kdim), lambda j, i: (0, 0)))
        args.append(gain_in.reshape(1, kdim))
    out_specs, out_shape, aliases = [], [], {}
    for k, (dt, stacked) in enumerate(outs):
        if stacked is None:
            out_specs.append(pl.BlockSpec((tm, tn), lambda j, i: (i, j)))
            out_shape.append(jax.ShapeDtypeStruct((n_rows, n), dt))
        else:
            out_specs.append(pl.BlockSpec((None, tm, tn), lambda j, i: (layer, i, j)))
            if isinstance(stacked, int):
                out_shape.append(jax.ShapeDtypeStruct((stacked, n_rows, n), dt))
            else:
                out_shape.append(jax.ShapeDtypeStruct(stacked.shape, dt))
                aliases[len(args)] = k
                in_specs.append(pl.BlockSpec(memory_space=pl.ANY))
                args.append(stacked)
    return pl.pallas_call(
        functools.partial(_proj_kernel, mode=mode, norm_in=gain_in is not None, n_out=len(outs)),
        grid=(n // tn, n_rows // tm),
        in_specs=in_specs,
        out_specs=out_specs,
        out_shape=out_shape,
        input_output_aliases=aliases,
        compiler_params=_cparams("parallel", "parallel"),
        name="proj_" + mode,
    )(*args)


def _cumsum_kernel(x_ref, ct_ref, carry_ref):
    @pl.when(pl.program_id(1) == 0)
    def _():
        carry_ref[...] = jnp.zeros_like(carry_ref)

    x = x_ref[...]
    tc = x.shape[0]
    tri = (lax.broadcasted_iota(jnp.int32, (tc, tc), 1)
           <= lax.broadcasted_iota(jnp.int32, (tc, tc), 0)).astype(F32)
    c = _dot_exact(tri, x) + carry_ref[...]
    ct_ref[...] = c.T[:8, :]
    carry_ref[...] = c[tc - 1:tc, :]


def _cumsum(logf, n_seq, seq_len):
    tc = _row_tile(seq_len, 512, LANES)
    nc = seq_len // tc
    return pl.pallas_call(
        _cumsum_kernel,
        grid=(n_seq, nc),
        in_specs=[pl.BlockSpec((tc, LANES), lambda b, c: (b * nc + c, 0))],
        out_specs=pl.BlockSpec((None, 8, tc), lambda b, c: (b, 0, c)),
        out_shape=jax.ShapeDtypeStruct((n_seq, 8, seq_len), F32),
        scratch_shapes=[pltpu.VMEM((1, LANES), F32)],
        compiler_params=_cparams("parallel", "arbitrary"),
        name="logf_cumsum",
    )(logf)


def _fox_prompt_kernel(qi_ref, kj_ref, q_ref, k_ref, v_ref, ct_ref, o_ref,
                       m_ref, l_ref, acc_ref, kb_ref, vb_ref, cj_ref, s_ref, p_ref, al_ref,
                       *, n_heads, scale, rc):
    t = pl.program_id(1)
    qi = qi_ref[t]
    kj = kj_ref[t]
    tq = q_ref.shape[0]
    tk = k_ref.shape[0]
    rep = tk // LANES

    @pl.when(kj == 0)
    def _():
        m_ref[...] = jnp.full(m_ref.shape, M_INIT, F32)
        l_ref[...] = jnp.zeros_like(l_ref)
        acc_ref[...] = jnp.zeros_like(acc_ref)

    kb_ref[...] = k_ref[...].astype(BF16)
    vb_ref[...] = v_ref[...].astype(BF16)
    for h in range(n_heads):
        cj_ref[h] = jnp.broadcast_to(ct_ref[h:h + 1, :] * LOG2E, (8, tk))

    def step(masked):
        if masked:
            diff = (lax.broadcasted_iota(jnp.int32, (rc, tk), 1)
                    - lax.broadcasted_iota(jnp.int32, (rc, tk), 0))
        for h in range(n_heads):
            sl = slice(h * HEAD_DIM, (h + 1) * HEAD_DIM)
            s_ref[...] = _dot_nt(q_ref[:, sl], kb_ref[:, sl])
            cj = pltpu.repeat(cj_ref[h], rc // 8, axis=0)
            for c in range(tq // rc):
                rows = slice(c * rc, (c + 1) * rc)
                s = s_ref[rows, :] * (scale * LOG2E) - cj
                if masked:
                    s = jnp.where(diff <= c * rc, s, NEG_INF)
                m_old = m_ref[h, rows, :]
                m_new = jnp.maximum(m_old, jnp.max(s, axis=1, keepdims=True))
                alpha = jnp.exp2(m_old - m_new)
                p = jnp.exp2(s - pltpu.repeat(m_new, rep, axis=1))
                l_ref[h, rows, :] = alpha * l_ref[h, rows, :] + jnp.sum(p, axis=1, keepdims=True)
                m_ref[h, rows, :] = m_new
                al_ref[rows, :] = alpha
                p_ref[rows, :] = p.astype(BF16)
            acc_ref[h] = al_ref[...] * acc_ref[h] + _dot(p_ref[...], vb_ref[:, sl])

    @pl.when(kj < qi)
    def _():
        step(False)

    @pl.when(kj == qi)
    def _():
        step(True)
        for h in range(n_heads):
            o_ref[:, h * HEAD_DIM:(h + 1) * HEAD_DIM] = (acc_ref[h] / l_ref[h]).astype(o_ref.dtype)


def _fox_prompt(q, k_all, v_all, ct, layer, n_seq, seq_len):
    aw = q.shape[1]
    n_heads = aw // HEAD_DIM
    tq = _row_tile(seq_len, 512, LANES)
    nq = seq_len // tq
    pairs = [(i, j) for i in range(nq) for j in range(i + 1)]
    qi_tab = jnp.asarray([p[0] for p in pairs], jnp.int32)
    kj_tab = jnp.asarray([p[1] for p in pairs], jnp.int32)
    grid_spec = pltpu.PrefetchScalarGridSpec(
        num_scalar_prefetch=2,
        grid=(n_seq, len(pairs)),
        in_specs=[
            pl.BlockSpec((tq, aw), lambda b, t, qi, kj: (b * nq + qi[t], 0)),
            pl.BlockSpec((None, tq, aw), lambda b, t, qi, kj: (layer, b * nq + kj[t], 0)),
            pl.BlockSpec((None, tq, aw), lambda b, t, qi, kj: (layer, b * nq + kj[t], 0)),
            pl.BlockSpec((None, 8, tq), lambda b, t, qi, kj: (b, 0, kj[t])),
        ],
        out_specs=pl.BlockSpec((tq, aw), lambda b, t, qi, kj: (b * nq + qi[t], 0)),
        scratch_shapes=[pltpu.VMEM((n_heads, tq, LANES), F32),
                        pltpu.VMEM((n_heads, tq, LANES), F32),
                        pltpu.VMEM((n_heads, tq, HEAD_DIM), F32),
                        pltpu.VMEM((tq, aw), BF16),
                        pltpu.VMEM((tq, aw), BF16),
                        pltpu.VMEM((n_heads, 8, tq), F32),
                        pltpu.VMEM((tq, tq), F32),
                        pltpu.VMEM((tq, tq), BF16),
                        pltpu.VMEM((tq, LANES), F32)],
    )
    return pl.pallas_call(
        functools.partial(_fox_prompt_kernel, n_heads=n_heads, scale=HEAD_DIM ** -0.5, rc=min(64, tq)),
        grid_spec=grid_spec,
        out_shape=jax.ShapeDtypeStruct((n_seq * seq_len, aw), BF16),
        compiler_params=_cparams("parallel", "arbitrary"),
        name="fox_prompt",
    )(qi_tab, kj_tab, q, k_all, v_all, ct)


def _fox_sample_kernel(pt_ref, q_ref, kn_ref, vn_ref, lfnt_ref, *rest, n_heads, n_pg, scale):
    lft_refs = rest[:n_pg]
    k_refs = rest[n_pg:2 * n_pg]
    v_refs = rest[2 * n_pg:3 * n_pg]
    o_ref, m_ref, l_ref, acc_ref, carry_ref, s_ref = rest[3 * n_pg:]
    del pt_ref
    s_idx = pl.program_id(1)
    n_steps = pl.num_programs(1)
    tn = q_ref.shape[0]
    rows = n_pg * n_heads

    @pl.when(s_idx == 0)
    def _():
        m_ref[...] = jnp.full(m_ref.shape, M_INIT, F32)
        l_ref[...] = jnp.zeros_like(l_ref)
        acc_ref[...] = jnp.zeros_like(acc_ref)
        carry_ref[...] = jnp.zeros_like(carry_ref)

    lft = jnp.concatenate([lft_refs[r][...] for r in range(n_pg)], axis=0)
    later = (lax.broadcasted_iota(jnp.int32, (PAGE_SIZE, PAGE_SIZE), 0)
             > lax.broadcasted_iota(jnp.int32, (PAGE_SIZE, PAGE_SIZE), 1)).astype(F32)
    page_sum = _dot_exact(lft, jnp.ones((PAGE_SIZE, LANES), F32))
    ra = lax.broadcasted_iota(jnp.int32, (rows, rows), 0)
    rb = lax.broadcasted_iota(jnp.int32, (rows, rows), 1)
    same_head = (ra & (n_heads - 1)) == (rb & (n_heads - 1))
    later_pages = jnp.where(same_head & (rb > ra), 1.0, 0.0)
    carry = carry_ref[...]
    carry_rows = jnp.concatenate([carry] * n_pg, axis=0)
    rev = (_dot_exact(lft, later) + _dot_exact(later_pages, page_sum) + carry_rows) * LOG2E
    ha = lax.broadcasted_iota(jnp.int32, (n_heads, rows), 0)
    hb = lax.broadcasted_iota(jnp.int32, (n_heads, rows), 1)
    per_head = jnp.where(ha == (hb & (n_heads - 1)), 1.0, 0.0)
    carry_ref[...] = carry + _dot_exact(per_head, page_sum)

    for h in range(n_heads):
        sl = slice(h * HEAD_DIM, (h + 1) * HEAD_DIM)
        qh = q_ref[:, sl].astype(BF16)
        for r in range(n_pg):
            kh = k_refs[r][pl.ds(h, PAGE_SIZE, stride=n_heads), :].astype(BF16)
            row = r * n_heads + h
            s_ref[h * tn:(h + 1) * tn, r * PAGE_SIZE:(r + 1) * PAGE_SIZE] = (
                _dot_nt(qh, kh) * (scale * LOG2E) + rev[row:row + 1, :])
    sc = s_ref[...]
    m_old = m_ref[...]
    m_new = jnp.maximum(m_old, jnp.max(sc, axis=1, keepdims=True))
    alpha = jnp.exp2(m_old - m_new)
    p = jnp.exp2(sc - pltpu.repeat(m_new, n_pg, axis=1))
    l_ref[...] = alpha * l_ref[...] + jnp.sum(p, axis=1, keepdims=True)
    m_ref[...] = m_new
    s_ref[...] = p
    for h in range(n_heads):
        hr = slice(h * tn, (h + 1) * tn)
        acc = alpha[hr, :] * acc_ref[hr, :]
        for r in range(n_pg):
            vh = v_refs[r][pl.ds(h, PAGE_SIZE, stride=n_heads), :].astype(BF16)
            acc = acc + _dot(s_ref[hr, r * PAGE_SIZE:(r + 1) * PAGE_SIZE].astype(BF16), vh)
        acc_ref[hr, :] = acc

    @pl.when(s_idx == n_steps - 1)
    def _():
        upto = (lax.broadcasted_iota(jnp.int32, (LANES, LANES), 0)
                <= lax.broadcasted_iota(jnp.int32, (LANES, LANES), 1)).astype(F32)
        cq_t = _dot_exact(lfnt_ref[...], upto) * LOG2E
        keep = (lax.broadcasted_iota(jnp.int32, (tn, LANES), 1)
                <= lax.broadcasted_iota(jnp.int32, (tn, LANES), 0))
        pad = jnp.zeros((LANES - tn, HEAD_DIM), F32)
        for h in range(n_heads):
            sl = slice(h * HEAD_DIM, (h + 1) * HEAD_DIM)
            hr = slice(h * tn, (h + 1) * tn)
            qh = q_ref[:, sl].astype(BF16)
            kh = jnp.concatenate([kn_ref[:, sl], pad], axis=0).astype(BF16)
            vh = jnp.concatenate([vn_ref[:, sl], pad], axis=0).astype(BF16)
            s = _dot_nt(qh, kh) * (scale * LOG2E) - cq_t[h:h + 1, :]
            s = jnp.where(keep, s, NEG_INF)
            m_o = m_ref[hr, :]
            m_n = jnp.maximum(m_o, jnp.max(s, axis=1, keepdims=True))
            al = jnp.exp2(m_o - m_n)
            pn = jnp.exp2(s - m_n)
            l_n = al * l_ref[hr, :] + jnp.sum(pn, axis=1, keepdims=True)
            o_ref[:, sl] = (al * acc_ref[hr, :] + _dot(pn.astype(BF16), vh)) / l_n


def _fox_sample(q, k_all, v_all, logf_t, cache_k, cache_v, cache_lft, page_table, layer, n_seq, n_new):
    aw = q.shape[1]
    n_heads = aw // HEAD_DIM
    assert n_heads & (n_heads - 1) == 0
    n_pages = page_table.shape[1]
    n_pg = 8 if n_pages % 8 == 0 else (4 if n_pages % 4 == 0 else 1)
    n_steps = n_pages // n_pg

    def page_map(r):
        def f(b, s, pt):
            return (layer, pt[b, (n_steps - 1 - s) * n_pg + r], 0, 0)
        return f

    in_specs = [
        pl.BlockSpec((n_new, aw), lambda b, s, pt: (b, 0)),
        pl.BlockSpec((None, n_new, aw), lambda b, s, pt: (layer, b, 0)),
        pl.BlockSpec((None, n_new, aw), lambda b, s, pt: (layer, b, 0)),
        pl.BlockSpec((None, n_heads, LANES), lambda b, s, pt: (b, 0, 0)),
    ]
    in_specs += [pl.BlockSpec((None, None, n_heads, PAGE_SIZE), page_map(r)) for r in range(n_pg)]
    in_specs += [pl.BlockSpec((None, None, PAGE_SIZE * n_heads, HEAD_DIM), page_map(r))
                 for r in range(n_pg)]
    in_specs += [pl.BlockSpec((None, None, PAGE_SIZE * n_heads, HEAD_DIM), page_map(r))
                 for r in range(n_pg)]
    grid_spec = pltpu.PrefetchScalarGridSpec(
        num_scalar_prefetch=1,
        grid=(n_seq, n_steps),
        in_specs=in_specs,
        out_specs=pl.BlockSpec((n_new, aw), lambda b, s, pt: (b, 0)),
        scratch_shapes=[pltpu.VMEM((n_heads * n_new, LANES), F32),
                        pltpu.VMEM((n_heads * n_new, LANES), F32),
                        pltpu.VMEM((n_heads * n_new, HEAD_DIM), F32),
                        pltpu.VMEM((n_heads, LANES), F32),
                        pltpu.VMEM((n_heads * n_new, n_pg * PAGE_SIZE), F32)],
    )
    args = [page_table, q, k_all, v_all, logf_t] + [cache_lft] * n_pg + [cache_k] * n_pg + [cache_v] * n_pg
    return pl.pallas_call(
        functools.partial(_fox_sample_kernel, n_heads=n_heads, n_pg=n_pg, scale=HEAD_DIM ** -0.5),
        grid_spec=grid_spec,
        out_shape=jax.ShapeDtypeStruct((n_seq * n_new, aw), F32),
        compiler_params=_cparams("parallel", "arbitrary"),
        name="fox_sample",
    )(*args)


def _cmul(a1, a2, x):
    return a1 * x + a2 * pltpu.roll(x, STATE_N, axis=1)


def _chunk_scan(x, ap, n_chunks):
    row = lax.broadcasted_iota(jnp.int32, x.shape, 0)
    hs = x
    sh, lvl = 1, 0
    while sh < n_chunks:
        prev = jnp.where(row >= sh, pltpu.roll(hs, sh, axis=0), 0.0)
        hs = hs + _cmul(ap[2 * lvl:2 * lvl + 1, :], ap[2 * lvl + 1:2 * lvl + 2, :], prev)
        sh *= 2
        lvl += 1
    return hs, jnp.where(row >= 1, pltpu.roll(hs, 1, axis=0), 0.0)


def _ssm_prompt_kernel(s_ref, ws_ref, wo_ref, ta_ref, tb_ref, dsk_ref, ap_ref, z_ref, hf_ref,
                       lhs_ref, hb_ref, *, chunk, n_chunks):
    half = chunk // 2 * LANES
    for i in range(chunk):
        lhs_ref[:, i * LANES:(i + 1) * LANES] = s_ref[pl.ds(i, n_chunks, stride=chunk), :].astype(BF16)
    x = _dot(lhs_ref[...], ws_ref[...])
    for g in range(SLAB_GROUPS):
        gs = slice(g * 2 * STATE_N, (g + 1) * 2 * STATE_N)
        hs, hb = _chunk_scan(x[:, gs], ap_ref[g], n_chunks)
        hb_ref[:, gs] = hb.astype(BF16)
        hf_ref[g:g + 1, :] = hs[n_chunks - 1:n_chunks, :]
    hbv = hb_ref[...]
    ya = _dot(lhs_ref[:, :half], ta_ref[...]) + _dot(hbv, wo_ref[:, :half])
    yb = _dot(lhs_ref[...], tb_ref[...]) + _dot(hbv, wo_ref[:, half:])
    dsk = dsk_ref[...]
    for i in range(chunk):
        src = ya if i < chunk // 2 else yb
        off = (i % (chunk // 2)) * LANES
        rows = pl.ds(i, n_chunks, stride=chunk)
        y = src[:, off:off + LANES] + s_ref[rows, :] * dsk
        z_ref[rows, :] = _gelu_tanh(y)


def _ssm_prompt(s, prm, layer, n_seq, seq_len, chunk):
    sw = s.shape[1]
    n_slabs = sw // LANES
    n_chunks = seq_len // chunk
    ws, wo, ta, tb, dsk, ap = prm
    wq = lambda q, b: (layer, q, 0, 0)
    return pl.pallas_call(
        functools.partial(_ssm_prompt_kernel, chunk=chunk, n_chunks=n_chunks),
        grid=(n_slabs, n_seq),
        in_specs=[
            pl.BlockSpec((seq_len, LANES), lambda q, b: (b, q)),
            pl.BlockSpec((None, None) + ws.shape[2:], wq),
            pl.BlockSpec((None, None) + wo.shape[2:], wq),
            pl.BlockSpec((None, None) + ta.shape[2:], wq),
            pl.BlockSpec((None, None) + tb.shape[2:], wq),
            pl.BlockSpec((None, None, 1, LANES), wq),
            pl.BlockSpec((None, SLAB_GROUPS) + ap.shape[2:], wq),
        ],
        out_specs=[pl.BlockSpec((seq_len, LANES), lambda q, b: (b, q)),
                   pl.BlockSpec((None, SLAB_GROUPS, 2 * STATE_N), lambda q, b: (b, q, 0))],
        out_shape=[jax.ShapeDtypeStruct(s.shape, F32),
                   jax.ShapeDtypeStruct((n_seq, sw // GROUP_CH, 2 * STATE_N), F32)],
        scratch_shapes=[pltpu.VMEM((n_chunks, chunk * LANES), BF16),
                        pltpu.VMEM((n_chunks, SLAB_GROUPS * 2 * STATE_N), BF16)],
        compiler_params=_cparams("parallel", "parallel"),
        name="s5_prompt",
    )(s, ws, wo, ta, tb, dsk, ap)


def _ssm_sample_kernel(u_ref, ws_ref, wo_ref, tp_ref, dsk_ref, ap_ref, h0_ref, z_ref, hf_ref, *, gb):
    for g in range(gb):
        u = u_ref[g]
        ub = u.astype(BF16)
        ap = ap_ref[g]
        hb = h0_ref[g]
        hf_ref[g] = _cmul(ap[0:1, :], ap[1:2, :], hb) + _dot(ub, ws_ref[g])
        y = _dot(ub, tp_ref[g]) + _dot(hb.astype(BF16), wo_ref[g]) + u * dsk_ref[g]
        z_ref[g] = _gelu_tanh(y)


def _ssm_sample(uf, prm, layer, h0):
    n_groups, rows, lc = uf.shape
    gb = 8 if n_groups % 8 == 0 else 1
    g3 = lambda i: (i, 0, 0)
    g4 = lambda i: (layer, i, 0, 0)
    return pl.pallas_call(
        functools.partial(_ssm_sample_kernel, gb=gb),
        grid=(n_groups // gb,),
        in_specs=([pl.BlockSpec((gb, rows, lc), g3)]
                  + [pl.BlockSpec((None, gb) + a.shape[2:], g4) for a in prm]
                  + [pl.BlockSpec((gb, rows, 2 * STATE_N), g3)]),
        out_specs=[pl.BlockSpec((gb, rows, lc), g3),
                   pl.BlockSpec((gb, rows, 2 * STATE_N), g3)],
        out_shape=[jax.ShapeDtypeStruct((n_groups, rows, lc), F32),
                   jax.ShapeDtypeStruct((n_groups, rows, 2 * STATE_N), F32)],
        compiler_params=_cparams("parallel"),
        name="s5_sample",
    )(uf, *prm, h0)


def _ssm_params(lam_re, lam_im, log_dt, b_re, b_im, c_re, c_im, d, chunk, n_chunks):
    hp = dict(precision=HIGHEST)
    dt = jnp.exp(log_dt)[:, None]
    mag = jnp.exp(lam_re * dt)
    a_re = mag * jnp.cos(lam_im * dt)
    a_im = mag * jnp.sin(lam_im * dt)
    den = lam_re * lam_re + lam_im * lam_im
    z_re = ((a_re - 1) * lam_re + a_im * lam_im) / den
    z_im = (a_im * lam_re - (a_re - 1) * lam_im) / den
    bb_re = z_re[..., None] * b_re - z_im[..., None] * b_im
    bb_im = z_re[..., None] * b_im + z_im[..., None] * b_re
    pr, pi = [jnp.ones_like(a_re)], [jnp.zeros_like(a_im)]
    for _ in range(chunk):
        pr.append(pr[-1] * a_re - pi[-1] * a_im)
        pi.append(pr[-2] * a_im + pi[-1] * a_re)
    pw_re = jnp.stack(pr)
    pw_im = jnp.stack(pi)
    dec_re = pw_re[chunk - 1::-1][:chunk]
    dec_im = pw_im[chunk - 1::-1][:chunk]
    ws_re = jnp.einsum('jgn,gnc->gjcn', dec_re, bb_re) - jnp.einsum('jgn,gnc->gjcn', dec_im, bb_im)
    ws_im = jnp.einsum('jgn,gnc->gjcn', dec_re, bb_im) + jnp.einsum('jgn,gnc->gjcn', dec_im, bb_re)
    ws = jnp.concatenate([ws_re, ws_im], axis=-1)
    ca_re = c_re[None] * pw_re[1:, :, None, :] - c_im[None] * pw_im[1:, :, None, :]
    ca_im = c_re[None] * pw_im[1:, :, None, :] + c_im[None] * pw_re[1:, :, None, :]
    wo = jnp.concatenate([ca_re, -ca_im], axis=-1).transpose(1, 3, 0, 2)
    k0_re = c_re[None] * pw_re[:chunk, :, None, :] - c_im[None] * pw_im[:chunk, :, None, :]
    k0_im = c_re[None] * pw_im[:chunk, :, None, :] + c_im[None] * pw_re[:chunk, :, None, :]
    km = (jnp.einsum('mgcn,gnd->gmcd', k0_re, bb_re, **hp)
          - jnp.einsum('mgcn,gnd->gmcd', k0_im, bb_im, **hp))
    lag = np.arange(chunk)[None, :] - np.arange(chunk)[:, None]
    tp = km[:, np.clip(lag, 0, chunk - 1)]
    tp = jnp.where((lag >= 0)[None, :, :, None, None], tp, 0.0).transpose(0, 1, 4, 2, 3)
    qr, qi = pw_re[chunk], pw_im[chunk]
    rows = []
    sh = 1
    while True:
        rows.append(jnp.concatenate([qr, qr], axis=-1))
        rows.append(jnp.concatenate([-qi, qi], axis=-1))
        sh *= 2
        if sh >= n_chunks:
            break
        qr, qi = qr * qr - qi * qi, 2 * qr * qi
    ap = jnp.stack(rows, axis=1)
    pad = (-ap.shape[1]) % 8
    if pad:
        ap = jnp.concatenate([ap, jnp.zeros((ap.shape[0], pad, 2 * STATE_N), F32)], axis=1)
    return ws, wo, tp, d, ap


def _group_form(prm, chunk):
    ws, wo, tp, d, ap = prm
    g = ws.shape[0]
    lc = chunk * GROUP_CH
    dsk = jnp.tile(d[:, None, :], (1, chunk, 1)).reshape(g, 1, lc)
    return (ws.reshape(g, lc, 2 * STATE_N).astype(BF16), wo.reshape(g, 2 * STATE_N, lc).astype(BF16),
            tp.reshape(g, lc, lc).astype(BF16), dsk, ap)


def _slab_form(prm, chunk):
    ws, wo, tp, d, ap = prm
    g = ws.shape[0]
    q = g // SLAB_GROUPS
    eye = jnp.eye(SLAB_GROUPS, dtype=BF16)
    lw = chunk * LANES
    sn = SLAB_GROUPS * 2 * STATE_N
    ws6 = ws.astype(BF16).reshape(q, SLAB_GROUPS, chunk, GROUP_CH, 2 * STATE_N)
    ws_s = (ws6.transpose(0, 2, 1, 3, 4)[:, :, :, :, None, :]
            * eye[None, None, :, None, :, None]).reshape(q, lw, sn)
    wo6 = wo.astype(BF16).reshape(q, SLAB_GROUPS, 2 * STATE_N, chunk, GROUP_CH)
    wo_s = (wo6[:, :, :, :, None, :] * eye[None, :, None, None, :, None]).reshape(q, sn, lw)
    tp6 = tp.astype(BF16).reshape(q, SLAB_GROUPS, chunk, GROUP_CH, chunk, GROUP_CH)

    def expand(t):
        nj, ni = t.shape[2], t.shape[4]
        return (t.transpose(0, 2, 1, 3, 4, 5)[:, :, :, :, :, None, :]
                * eye[None, None, :, None, None, :, None]).reshape(q, nj * LANES, ni * LANES)

    hc = chunk // 2
    dsk = d.reshape(q, 1, LANES)
    return ws_s, wo_s, expand(tp6[:, :, :hc, :, :hc, :]), expand(tp6[:, :, :, :, hc:, :]), dsk, ap


def _mix_kernel(h_ref, a_ref, z_ref, ga_ref, gb_ref, wp_ref, wa_ref, wb_ref, wo_ref, *rest):
    o_ref = rest[-1]
    o_att = _dot(a_ref[...].astype(BF16), wp_ref[...])
    zb = z_ref[...].astype(BF16)
    o_ssm = _dot(zb, wa_ref[...]) * jax.nn.sigmoid(_dot(zb, wb_ref[...]))
    g = (ga_ref[...] * o_att + gb_ref[...] * o_ssm).astype(BF16)
    o_ref[...] = h_ref[...] + _dot(g, wo_ref[...])


def _mix(h, a, z, ga, gb, wp, wa, wb, wo, layer, row0, tm, out_buf=None):
    m, d = h.shape
    n_rows = a.shape[0]
    rb0 = row0 // tm
    row = lambda w: pl.BlockSpec((tm, w), lambda i: (i, 0))
    mrow = pl.BlockSpec((tm, d), lambda i: (rb0 + i, 0))
    in_specs = [mrow, row(a.shape[1]), row(z.shape[1]), row(d), row(d),
                _layer_resident(wp, layer), _layer_resident(wa, layer),
                _layer_resident(wb, layer), _layer_resident(wo, layer)]
    args = [h, a, z, ga, gb, wp, wa, wb, wo]
    aliases = {}
    if out_buf is not None:
        aliases[len(args)] = 0
        in_specs.append(pl.BlockSpec(memory_space=pl.ANY))
        args.append(out_buf)
    return pl.pallas_call(
        _mix_kernel,
        grid=(n_rows // tm,),
        in_specs=in_specs,
        out_specs=mrow,
        out_shape=jax.ShapeDtypeStruct((m, d), F32),
        input_output_aliases=aliases,
        compiler_params=_cparams("parallel"),
        name="mix",
    )(*args)


def _xattn_kernel(h_ref, g_ref, wq_ref, qg_ref, mk_ref, mv_ref, wo_ref, o_ref, *, n_heads, scale):
    x = h_ref[...]
    u = _rms(x, g_ref[...]).astype(BF16)
    qx = _dot(u, wq_ref[...])
    outs = []
    for h in range(n_heads):
        sl = slice(h * X_HEAD_DIM, (h + 1) * X_HEAD_DIM)
        qh = _rms(qx[:, sl], qg_ref[...]).astype(BF16)
        s = _dot_nt(qh, mk_ref[:, sl].astype(BF16)) * scale
        e = jnp.exp(s - jnp.max(s, axis=1, keepdims=True))
        p = e / jnp.sum(e, axis=1, keepdims=True)
        outs.append(_dot(p.astype(BF16), mv_ref[:, sl].astype(BF16)))
    o = jnp.concatenate(outs, axis=1).astype(BF16)
    o_ref[...] = x + _dot(o, wo_ref[...])


def _xattn(h, g, wq, qg, mk_all, mv_all, wo, layer, n_seq, seq_len, row0, tm):
    m, d = h.shape
    xw = wq.shape[2]
    n_mem = mk_all.shape[1] // n_seq
    nt = seq_len // tm
    blk0 = row0 // tm
    rows = lambda b, i: (blk0 + b * nt + i, 0)
    small = lambda shape: pl.BlockSpec(shape, lambda b, i: (0,) * len(shape), pipeline_mode=pl.Buffered(1))
    return pl.pallas_call(
        functools.partial(_xattn_kernel, n_heads=xw // X_HEAD_DIM, scale=X_HEAD_DIM ** -0.5),
        grid=(n_seq, nt),
        in_specs=[pl.BlockSpec((tm, d), rows),
                  small((1, d)), _layer_resident(wq, layer), small((1, X_HEAD_DIM)),
                  pl.BlockSpec((None, n_mem, xw), lambda b, i: (layer, b, 0)),
                  pl.BlockSpec((None, n_mem, xw), lambda b, i: (layer, b, 0)),
                  _layer_resident(wo, layer)],
        out_specs=pl.BlockSpec((tm, d), rows),
        out_shape=jax.ShapeDtypeStruct((m, d), F32),
        input_output_aliases={0: 0},
        compiler_params=_cparams("parallel", "arbitrary"),
        name="xattn",
    )(h, g.reshape(1, d), wq, qg.reshape(1, X_HEAD_DIM), mk_all, mv_all, wo)


def kernel(x_prompt, x_sample, cache_k, cache_v, cache_logf, state_ssm_re, state_ssm_im, cache_mem_k, cache_mem_v, page_table, mem_prompt, ffn1_norm, ffn1_w_gate, ffn1_w_up, ffn1_w_down, mix_norm, w_in, b_forget, q_norm, k_norm, ssm_lambda_re, ssm_lambda_im, ssm_log_dt, ssm_b_re, ssm_b_im, ssm_c_re, ssm_c_im, ssm_d, ssm_glu_w, ssm_glu_v, w_att_proj, w_out, cross_norm, mem_norm, w_cq, w_ck, w_cv, cq_norm, ck_norm, w_co, ffn2_norm, ffn2_w_gate, ffn2_w_up, ffn2_w_down):
    bp, t_p, d = x_prompt.shape
    bd, t_s, _ = x_sample.shape
    depth = ffn1_norm.shape[0]
    aw = w_att_proj.shape[1]
    n_heads = aw // HEAD_DIM
    sw = ssm_glu_w.shape[1]
    n_groups = sw // GROUP_CH
    xw = w_cq.shape[2]
    x_heads = xw // X_HEAD_DIM
    n_mem = mem_prompt.shape[1]
    n_pool = cache_k.shape[1]
    mp, ms = bp * t_p, bd * t_s
    mt = mp + ms
    chunk_p = 16
    nc_p = t_p // chunk_p
    tm_p = _row_tile(t_p, 1024, LANES)
    tm_mix = _row_tile(mp, 256)
    assert n_heads <= 8 and t_s == 8 and t_p % chunk_p == 0 and mp % ms == 0 and sw % LANES == 0

    x = jnp.concatenate([x_prompt.reshape(mp, d), x_sample.reshape(ms, d)], axis=0)
    mem = mem_prompt.reshape(bp * n_mem, d)
    ck = cache_k.reshape(depth, n_pool, PAGE_SIZE * n_heads, HEAD_DIM)
    cv = cache_v.reshape(depth, n_pool, PAGE_SIZE * n_heads, HEAD_DIM)
    clft = cache_logf.transpose(0, 1, 3, 2)
    cmk = cache_mem_k.reshape(depth, bd * n_mem, xw)
    cmv = cache_mem_v.reshape(depth, bd * n_mem, xw)

    o_f = 3 * aw
    o_s = o_f + n_heads
    bf = lambda w: w.astype(BF16)
    w_qkv = bf(w_in[:, :, :o_f])
    w_f = bf(jnp.pad(w_in[:, :, o_f:o_s], ((0, 0), (0, 0), (0, LANES - n_heads))))
    w_sg = bf(w_in[:, :, o_s:])
    b_f = jnp.pad(b_forget, ((0, 0), (0, LANES - n_heads)))
    f1g, f1u, f1d = bf(ffn1_w_gate), bf(ffn1_w_up), bf(ffn1_w_down)
    f2g, f2u, f2d = bf(ffn2_w_gate), bf(ffn2_w_up), bf(ffn2_w_down)
    wp_b, wa_b, wb_b, wo_b = bf(w_att_proj), bf(ssm_glu_w), bf(ssm_glu_v), bf(w_out)
    wcq_b, wck_b, wcv_b, wco_b = bf(w_cq), bf(w_ck), bf(w_cv), bf(w_co)

    ssm_raw = (ssm_lambda_re, ssm_lambda_im, ssm_log_dt, ssm_b_re, ssm_b_im, ssm_c_re, ssm_c_im, ssm_d)
    prm_p_all = jax.vmap(lambda *a: _slab_form(_ssm_params(*a, chunk_p, nc_p), chunk_p))(*ssm_raw)
    prm_s_all = jax.vmap(lambda *a: _group_form(_ssm_params(*a, t_s, 1), t_s))(*ssm_raw)

    kp = vp = ks = vs = mkp = mvp = None
    lf_p_l, lf_s_l, hf_p_l, hf_s_l = [], [], [], []
    for l in range(depth):
        first = l == 0
        st = lambda buf: depth if first else buf

        h1, u = _ffn(x, ffn1_norm[l], f1g, f1u, f1d, l, g_next=mix_norm[l])

        qg = q_norm[l].reshape(1, HEAD_DIM)
        kg = k_norm[l].reshape(1, HEAD_DIM)
        pr = dict(row0=0, n_rows=mp, tm=tm_p)
        sr = dict(row0=mp, n_rows=ms, tm=ms)
        (q_p,) = _proj(u, w_qkv, l, 0, aw, mode="headnorm", aux=qg, outs=((BF16, None),), **pr)
        (q_s,) = _proj(u, w_qkv, l, 0, aw, mode="headnorm", aux=qg, **sr)
        (kp,) = _proj(u, w_qkv, l, aw, aw, mode="headnorm", aux=kg, outs=((F32, st(kp)),), **pr)
        (ks,) = _proj(u, w_qkv, l, aw, aw, mode="headnorm", aux=kg, outs=((F32, st(ks)),), **sr)
        (vp,) = _proj(u, w_qkv, l, 2 * aw, aw, outs=((F32, st(vp)),), **pr)
        (vs,) = _proj(u, w_qkv, l, 2 * aw, aw, outs=((F32, st(vs)),), **sr)
        (lf_p,) = _proj(u, w_f, l, 0, LANES, mode="logsigmoid", aux=b_f[l:l + 1], **pr)
        (lf_s,) = _proj(u, w_f, l, 0, LANES, mode="logsigmoid", aux=b_f[l:l + 1], **sr)
        (s_p,) = _proj(u, w_sg, l, 0, sw, **pr)
        (s_s,) = _proj(u, w_sg, l, 0, sw, **sr)
        (ga_p,) = _proj(u, w_sg, l, sw, d, mode="sigmoid", **pr)
        (ga_s,) = _proj(u, w_sg, l, sw, d, mode="sigmoid", **sr)
        (gb_p,) = _proj(u, w_sg, l, sw + d, d, mode="sigmoid", **pr)
        (gb_s,) = _proj(u, w_sg, l, sw + d, d, mode="sigmoid", **sr)

        ct = _cumsum(lf_p, bp, t_p)
        att_p = _fox_prompt(q_p, kp, vp, ct, l, bp, t_p)
        lf_st = lf_s.reshape(bd, t_s, LANES)[:, :, :n_heads].transpose(0, 2, 1)
        lf_st = jnp.pad(lf_st, ((0, 0), (0, 0), (0, LANES - t_s)))
        att_s = _fox_sample(q_s, ks, vs, lf_st, ck, cv, clft, page_table, l, bd, t_s)

        z_p, hf_p = _ssm_prompt(s_p, prm_p_all, l, bp, t_p, chunk_p)
        uf_s = (s_s.reshape(bd, t_s, n_groups, GROUP_CH).transpose(2, 0, 1, 3)
                .reshape(n_groups, bd, t_s * GROUP_CH))
        h0 = jnp.concatenate([state_ssm_re[l], state_ssm_im[l]], axis=-1).transpose(1, 0, 2)
        zf_s, hf_s = _ssm_sample(uf_s, prm_s_all, l, h0)
        z_s = zf_s.reshape(n_groups, bd, t_s, GROUP_CH).transpose(1, 2, 0, 3).reshape(ms, sw)

        h2 = _mix(h1, att_p, z_p, ga_p, gb_p, wp_b, wa_b, wb_b, wo_b, l, 0, tm_mix)
        h2 = _mix(h1, att_s, z_s, ga_s, gb_s, wp_b, wa_b, wb_b, wo_b, l, mp, ms, out_buf=h2)

        mr = dict(row0=0, n_rows=bp * n_mem, tm=_row_tile(bp * n_mem, 512), gain_in=mem_norm[l])
        (mkp,) = _proj(mem, wck_b, l, 0, xw, mode="headnorm", aux=ck_norm[l].reshape(1, X_HEAD_DIM),
                       outs=((F32, st(mkp)),), **mr)
        (mvp,) = _proj(mem, wcv_b, l, 0, xw, outs=((F32, st(mvp)),), **mr)
        h3 = _xattn(h2, cross_norm[l], wcq_b, cq_norm[l], mkp, mvp, wco_b, l, bp, t_p, 0,
                    _row_tile(t_p, 512, LANES))
        h3 = _xattn(h3, cross_norm[l], wcq_b, cq_norm[l], cmk, cmv, wco_b, l, bd, t_s, mp, t_s)

        x = _ffn(h3, ffn2_norm[l], f2g, f2u, f2d, l)

        lf_p_l.append(lf_p)
        lf_s_l.append(lf_s)
        hf_p_l.append(hf_p)
        hf_s_l.append(hf_s)

    lf_p_all = jnp.stack(lf_p_l)[:, :, :n_heads]
    lf_s_all = jnp.stack(lf_s_l)[:, :, :n_heads]
    hf_p_all = jnp.stack(hf_p_l)
    hf_s_all = jnp.stack(hf_s_l).transpose(0, 2, 1, 3)
    return (x[:mp].reshape(bp, t_p, d), x[mp:].reshape(bd, t_s, d),
            kp.reshape(depth, bp, t_p, n_heads, HEAD_DIM), vp.reshape(depth, bp, t_p, n_heads, HEAD_DIM),
            lf_p_all.reshape(depth, bp, t_p, n_heads),
            hf_p_all[..., :STATE_N], hf_p_all[..., STATE_N:],
            mkp.reshape(depth, bp, n_mem, x_heads, X_HEAD_DIM), mvp.reshape(depth, bp, n_mem, x_heads, X_HEAD_DIM),
            ks.reshape(depth, bd, t_s, n_heads, HEAD_DIM), vs.reshape(depth, bd, t_s, n_heads, HEAD_DIM),
            lf_s_all.reshape(depth, bd, t_s, n_heads),
            hf_s_all[..., :STATE_N], hf_s_all[..., STATE_N:])
```

```python
import functools
import math

import jax
import jax.numpy as jnp
import numpy as np
from jax import lax
from jax.experimental import pallas as pl
from jax.experimental.pallas import tpu as pltpu

F32 = jnp.float32
BF16 = jnp.bfloat16
EPS = 1e-6
LANES = 128
HEAD_DIM = 128
GROUP_CH = 16
STATE_N = 64
PAGE_SIZE = 128
X_HEAD_DIM = 128
SLAB_GROUPS = LANES // GROUP_CH
NEG_INF = float("-inf")
M_INIT = -1e30
LOG2E = math.log2(math.e)
VMEM_LIMIT = 56 * 1024 * 1024
HIGHEST = lax.Precision.HIGHEST


def _cparams(*sem):
    return pltpu.CompilerParams(dimension_semantics=sem, vmem_limit_bytes=VMEM_LIMIT)


def _layer_resident(w, layer):
    return pl.BlockSpec((None,) + w.shape[1:], lambda *_: (layer, 0, 0), pipeline_mode=pl.Buffered(1))


def _row_tile(m, target, mult=16):
    best = None
    for t in range(mult, min(m, target) + 1, mult):
        if m % t == 0:
            best = t
    assert best is not None, (m, target)
    return best


def _rms(x, g):
    return (x * lax.rsqrt(jnp.mean(x * x, axis=-1, keepdims=True) + EPS)) * g


def _dot(a, b):
    return jnp.dot(a, b, preferred_element_type=F32)


def _dot_nt(a, b):
    return lax.dot_general(a, b, (((1,), (1,)), ((), ())), preferred_element_type=F32)


def _dot_exact(a, b):
    return jnp.dot(a, b, preferred_element_type=F32, precision=HIGHEST)


def _log_sigmoid(x):
    return -(jnp.maximum(-x, 0.0) + jnp.log1p(jnp.exp(-jnp.abs(x))))


def _gelu_tanh(x):
    return 0.5 * x * (1.0 + jnp.tanh(math.sqrt(2.0 / math.pi) * (x + 0.044715 * (x * x * x))))


def _ffn_kernel(x_ref, g_ref, wg_ref, wu_ref, wd_ref, *rest, n_ff, with_norm_out):
    if with_norm_out:
        g2_ref, o_ref, u_ref, h_ref = rest
    else:
        o_ref, h_ref = rest
    j = pl.program_id(1)

    @pl.when(j == 0)
    def _():
        h_ref[...] = _rms(x_ref[...], g_ref[...]).astype(BF16)

    h = h_ref[...]
    gate = _dot(h, wg_ref[...])
    up = _dot(h, wu_ref[...])
    act = (gate * jax.nn.sigmoid(gate) * up).astype(BF16)
    part = _dot(act, wd_ref[...])

    @pl.when(j == 0)
    def _():
        o_ref[...] = part

    @pl.when(j > 0)
    def _():
        o_ref[...] += part

    @pl.when(j == n_ff - 1)
    def _():
        y = x_ref[...] + 0.5 * o_ref[...]
        o_ref[...] = y
        if with_norm_out:
            u_ref[...] = _rms(y, g2_ref[...]).astype(BF16)


def _ffn(x, g, wg, wu, wd, layer, g_next=None):
    m, d = x.shape
    ff = wg.shape[2]
    tm = _row_tile(m, 704)
    tf = _row_tile(ff, 512, LANES)
    n_ff = ff // tf
    with_norm_out = g_next is not None
    in_specs = [
        pl.BlockSpec((tm, d), lambda i, j: (i, 0)),
        pl.BlockSpec((1, d), lambda i, j: (0, 0)),
        pl.BlockSpec((None, d, tf), lambda i, j: (layer, 0, j)),
        pl.BlockSpec((None, d, tf), lambda i, j: (layer, 0, j)),
        pl.BlockSpec((None, tf, d), lambda i, j: (layer, j, 0)),
    ]
    args = [x, g.reshape(1, d), wg, wu, wd]
    out_shape = [jax.ShapeDtypeStruct((m, d), F32)]
    out_specs = [pl.BlockSpec((tm, d), lambda i, j: (i, 0))]
    if with_norm_out:
        in_specs.append(pl.BlockSpec((1, d), lambda i, j: (0, 0)))
        args.append(g_next.reshape(1, d))
        out_shape.append(jax.ShapeDtypeStruct((m, d), BF16))
        out_specs.append(pl.BlockSpec((tm, d), lambda i, j: (i, 0)))
    res = pl.pallas_call(
        functools.partial(_ffn_kernel, n_ff=n_ff, with_norm_out=with_norm_out),
        grid=(m // tm, n_ff),
        in_specs=in_specs,
        out_specs=out_specs,
        out_shape=out_shape,
        scratch_shapes=[pltpu.VMEM((tm, d), BF16)],
        compiler_params=_cparams("parallel", "arbitrary"),
        name="ffn",
    )(*args)
    return res if with_norm_out else res[0]


def _head_norm(r, g):
    cols = []
    for h in range(r.shape[1] // HEAD_DIM):
        cols.append(_rms(r[:, h * HEAD_DIM:(h + 1) * HEAD_DIM], g))
    return cols[0] if len(cols) == 1 else jnp.concatenate(cols, axis=1)


def _proj_kernel(u_ref, w_ref, aux_ref, *rest, mode, norm_in, n_out):
    if norm_in:
        gin_ref, rest = rest[0], rest[1:]
        u = _rms(u_ref[...], gin_ref[...]).astype(BF16)
    else:
        u = u_ref[...]
    outs = rest[len(rest) - n_out:]
    r = _dot(u, w_ref[...])
    if mode == "headnorm":
        r = _head_norm(r, aux_ref[...])
    elif mode == "sigmoid":
        r = jax.nn.sigmoid(r)
    elif mode == "logsigmoid":
        r = _log_sigmoid(r + aux_ref[...])
    for o in outs:
        o[...] = r.astype(o.dtype)


def _proj(u, w, layer, col0, n, row0, n_rows, tm, mode="plain", aux=None, outs=((F32, None),),
          gain_in=None):
    kdim = u.shape[1]
    tn = _row_tile(math.gcd(n, col0), 1024, LANES)
    assert row0 % tm == 0 and n_rows % tm == 0
    rb0, cb0 = row0 // tm, col0 // tn
    if aux is None:
        aux = jnp.zeros((1, LANES), F32)
    in_specs = [
        pl.BlockSpec((tm, kdim), lambda j, i: (rb0 + i, 0)),
        pl.BlockSpec((None, kdim, tn), lambda j, i: (layer, 0, cb0 + j)),
        pl.BlockSpec(aux.shape, lambda j, i: (0, 0)),
    ]
    args = [u, w, aux]
    if gain_in is not None:
        in_specs.append(pl.BlockSpec((1, kdim), lambda j, i: (0, 0)))
        args.append(gain_in.reshape(1, kdim))
    out_specs, out_shape, aliases = [], [], {}
    for k, (dt, stacked) in enumerate(outs):
        if stacked is None:
            out_specs.append(pl.BlockSpec((tm, tn), lambda j, i: (i, j)))
            out_shape.append(jax.ShapeDtypeStruct((n_rows, n), dt))
        else:
            out_specs.append(pl.BlockSpec((None, tm, tn), lambda j, i: (layer, i, j)))
            if isinstance(stacked, int):
                out_shape.append(jax.ShapeDtypeStruct((stacked, n_rows, n), dt))
            else:
                out_shape.append(jax.ShapeDtypeStruct(stacked.shape, dt))
                aliases[len(args)] = k
                in_specs.append(pl.BlockSpec(memory_space=pl.ANY))
                args.append(stacked)
    return pl.pallas_call(
        functools.partial(_proj_kernel, mode=mode, norm_in=gain_in is not None, n_out=len(outs)),
        grid=(n // tn, n_rows // tm),
        in_specs=in_specs,
        out_specs=out_specs,
        out_shape=out_shape,
        input_output_aliases=aliases,
        compiler_params=_cparams("parallel", "parallel"),
        name="proj_" + mode,
    )(*args)


def _cumsum_kernel(x_ref, ct_ref, carry_ref):
    @pl.when(pl.program_id(1) == 0)
    def _():
        carry_ref[...] = jnp.zeros_like(carry_ref)

    x = x_ref[...]
    tc = x.shape[0]
    tri = (lax.broadcasted_iota(jnp.int32, (tc, tc), 1)
           <= lax.broadcasted_iota(jnp.int32, (tc, tc), 0)).astype(F32)
    c = _dot_exact(tri, x) + carry_ref[...]
    ct_ref[...] = c.T[:8, :]
    carry_ref[...] = c[tc - 1:tc, :]


def _cumsum(logf, n_seq, seq_len):
    tc = _row_tile(seq_len, 512, LANES)
    nc = seq_len // tc
    return pl.pallas_call(
        _cumsum_kernel,
        grid=(n_seq, nc),
        in_specs=[pl.BlockSpec((tc, LANES), lambda b, c: (b * nc + c, 0))],
        out_specs=pl.BlockSpec((None, 8, tc), lambda b, c: (b, 0, c)),
        out_shape=jax.ShapeDtypeStruct((n_seq, 8, seq_len), F32),
        scratch_shapes=[pltpu.VMEM((1, LANES), F32)],
        compiler_params=_cparams("parallel", "arbitrary"),
        name="logf_cumsum",
    )(logf)


def _fox_prompt_kernel(qi_ref, kj_ref, q_ref, k_ref, v_ref, ct_ref, o_ref,
                       m_ref, l_ref, acc_ref, kb_ref, vb_ref, cj_ref, s_ref, p_ref, al_ref,
                       *, n_heads, scale, rc):
    t = pl.program_id(1)
    qi = qi_ref[t]
    kj = kj_ref[t]
    tq = q_ref.shape[0]
    tk = k_ref.shape[0]
    rep = tk // LANES

    @pl.when(kj == 0)
    def _():
        m_ref[...] = jnp.full(m_ref.shape, M_INIT, F32)
        l_ref[...] = jnp.zeros_like(l_ref)
        acc_ref[...] = jnp.zeros_like(acc_ref)

    kb_ref[...] = k_ref[...].astype(BF16)
    vb_ref[...] = v_ref[...].astype(BF16)
    for h in range(n_heads):
        cj_ref[h] = jnp.broadcast_to(ct_ref[h:h + 1, :] * LOG2E, (8, tk))

    def step(masked):
        if masked:
            diff = (lax.broadcasted_iota(jnp.int32, (rc, tk), 1)
                    - lax.broadcasted_iota(jnp.int32, (rc, tk), 0))
        for h in range(n_heads):
            sl = slice(h * HEAD_DIM, (h + 1) * HEAD_DIM)
            s_ref[...] = _dot_nt(q_ref[:, sl], kb_ref[:, sl])
            cj = pltpu.repeat(cj_ref[h], rc // 8, axis=0)
            for c in range(tq // rc):
                rows = slice(c * rc, (c + 1) * rc)
                s = s_ref[rows, :] * (scale * LOG2E) - cj
                if masked:
                    s = jnp.where(diff <= c * rc, s, NEG_INF)
                m_old = m_ref[h, rows, :]
                m_new = jnp.maximum(m_old, jnp.max(s, axis=1, keepdims=True))
                alpha = jnp.exp2(m_old - m_new)
                p = jnp.exp2(s - pltpu.repeat(m_new, rep, axis=1))
                l_ref[h, rows, :] = alpha * l_ref[h, rows, :] + jnp.sum(p, axis=1, keepdims=True)
                m_ref[h, rows, :] = m_new
                al_ref[rows, :] = alpha
                p_ref[rows, :] = p.astype(BF16)
            acc_ref[h] = al_ref[...] * acc_ref[h] + _dot(p_ref[...], vb_ref[:, sl])

    @pl.when(kj < qi)
    def _():
        step(False)

    @pl.when(kj == qi)
    def _():
        step(True)
        for h in range(n_heads):
            o_ref[:, h * HEAD_DIM:(h + 1) * HEAD_DIM] = (acc_ref[h] / l_ref[h]).astype(o_ref.dtype)


def _fox_prompt(q, k_all, v_all, ct, layer, n_seq, seq_len):
    aw = q.shape[1]
    n_heads = aw // HEAD_DIM
    tq = _row_tile(seq_len, 512, LANES)
    nq = seq_len // tq
    pairs = [(i, j) for i in range(nq) for j in range(i + 1)]
    qi_tab = jnp.asarray([p[0] for p in pairs], jnp.int32)
    kj_tab = jnp.asarray([p[1] for p in pairs], jnp.int32)
    grid_spec = pltpu.PrefetchScalarGridSpec(
        num_scalar_prefetch=2,
        grid=(n_seq, len(pairs)),
        in_specs=[
            pl.BlockSpec((tq, aw), lambda b, t, qi, kj: (b * nq + qi[t], 0)),
            pl.BlockSpec((None, tq, aw), lambda b, t, qi, kj: (layer, b * nq + kj[t], 0)),
            pl.BlockSpec((None, tq, aw), lambda b, t, qi, kj: (layer, b * nq + kj[t], 0)),
            pl.BlockSpec((None, 8, tq), lambda b, t, qi, kj: (b, 0, kj[t])),
        ],
        out_specs=pl.BlockSpec((tq, aw), lambda b, t, qi, kj: (b * nq + qi[t], 0)),
        scratch_shapes=[pltpu.VMEM((n_heads, tq, LANES), F32),
                        pltpu.VMEM((n_heads, tq, LANES), F32),
                        pltpu.VMEM((n_heads, tq, HEAD_DIM), F32),
                        pltpu.VMEM((tq, aw), BF16),
                        pltpu.VMEM((tq, aw), BF16),
                        pltpu.VMEM((n_heads, 8, tq), F32),
                        pltpu.VMEM((tq, tq), F32),
                        pltpu.VMEM((tq, tq), BF16),
                        pltpu.VMEM((tq, LANES), F32)],
    )
    return pl.pallas_call(
        functools.partial(_fox_prompt_kernel, n_heads=n_heads, scale=HEAD_DIM ** -0.5, rc=min(64, tq)),
        grid_spec=grid_spec,
        out_shape=jax.ShapeDtypeStruct((n_seq * seq_len, aw), BF16),
        compiler_params=_cparams("parallel", "arbitrary"),
        name="fox_prompt",
    )(qi_tab, kj_tab, q, k_all, v_all, ct)


def _fox_sample_kernel(pt_ref, q_ref, kn_ref, vn_ref, lfnt_ref, *rest, n_heads, n_pg, scale):
    lft_refs = rest[:n_pg]
    k_refs = rest[n_pg:2 * n_pg]
    v_refs = rest[2 * n_pg:3 * n_pg]
    o_ref, m_ref, l_ref, acc_ref, carry_ref, s_ref = rest[3 * n_pg:]
    del pt_ref
    s_idx = pl.program_id(1)
    n_steps = pl.num_programs(1)
    tn = q_ref.shape[0]
    rows = n_pg * n_heads

    @pl.when(s_idx == 0)
    def _():
        m_ref[...] = jnp.full(m_ref.shape, M_INIT, F32)
        l_ref[...] = jnp.zeros_like(l_ref)
        acc_ref[...] = jnp.zeros_like(acc_ref)
        carry_ref[...] = jnp.zeros_like(carry_ref)

    lft = jnp.concatenate([lft_refs[r][...] for r in range(n_pg)], axis=0)
    later = (lax.broadcasted_iota(jnp.int32, (PAGE_SIZE, PAGE_SIZE), 0)
             > lax.broadcasted_iota(jnp.int32, (PAGE_SIZE, PAGE_SIZE), 1)).astype(F32)
    page_sum = _dot_exact(lft, jnp.ones((PAGE_SIZE, LANES), F32))
    ra = lax.broadcasted_iota(jnp.int32, (rows, rows), 0)
    rb = lax.broadcasted_iota(jnp.int32, (rows, rows), 1)
    same_head = (ra & (n_heads - 1)) == (rb & (n_heads - 1))
    later_pages = jnp.where(same_head & (rb > ra), 1.0, 0.0)
    carry = carry_ref[...]
    carry_rows = jnp.concatenate([carry] * n_pg, axis=0)
    rev = (_dot_exact(lft, later) + _dot_exact(later_pages, page_sum) + carry_rows) * LOG2E
    ha = lax.broadcasted_iota(jnp.int32, (n_heads, rows), 0)
    hb = lax.broadcasted_iota(jnp.int32, (n_heads, rows), 1)
    per_head = jnp.where(ha == (hb & (n_heads - 1)), 1.0, 0.0)
    carry_ref[...] = carry + _dot_exact(per_head, page_sum)

    for h in range(n_heads):
        sl = slice(h * HEAD_DIM, (h + 1) * HEAD_DIM)
        qh = q_ref[:, sl].astype(BF16)
        for r in range(n_pg):
            kh = k_refs[r][pl.ds(h, PAGE_SIZE, stride=n_heads), :].astype(BF16)
            row = r * n_heads + h
            s_ref[h * tn:(h + 1) * tn, r * PAGE_SIZE:(r + 1) * PAGE_SIZE] = (
                _dot_nt(qh, kh) * (scale * LOG2E) + rev[row:row + 1, :])
    sc = s_ref[...]
    m_old = m_ref[...]
    m_new = jnp.maximum(m_old, jnp.max(sc, axis=1, keepdims=True))
    alpha = jnp.exp2(m_old - m_new)
    p = jnp.exp2(sc - pltpu.repeat(m_new, n_pg, axis=1))
    l_ref[...] = alpha * l_ref[...] + jnp.sum(p, axis=1, keepdims=True)
    m_ref[...] = m_new
    s_ref[...] = p
    for h in range(n_heads):
        hr = slice(h * tn, (h + 1) * tn)
        acc = alpha[hr, :] * acc_ref[hr, :]
        for r in range(n_pg):
            vh = v_refs[r][pl.ds(h, PAGE_SIZE, stride=n_heads), :].astype(BF16)
            acc = acc + _dot(s_ref[hr, r * PAGE_SIZE:(r + 1) * PAGE_SIZE].astype(BF16), vh)
        acc_ref[hr, :] = acc

    @pl.when(s_idx == n_steps - 1)
    def _():
        upto = (lax.broadcasted_iota(jnp.int32, (LANES, LANES), 0)
                <= lax.broadcasted_iota(jnp.int32, (LANES, LANES), 1)).astype(F32)
        cq_t = _dot_exact(lfnt_ref[...], upto) * LOG2E
        keep = (lax.broadcasted_iota(jnp.int32, (tn, LANES), 1)
                <= lax.broadcasted_iota(jnp.int32, (tn, LANES), 0))
        pad = jnp.zeros((LANES - tn, HEAD_DIM), F32)
        for h in range(n_heads):
            sl = slice(h * HEAD_DIM, (h + 1) * HEAD_DIM)
            hr = slice(h * tn, (h + 1) * tn)
            qh = q_ref[:, sl].astype(BF16)
            kh = jnp.concatenate([kn_ref[:, sl], pad], axis=0).astype(BF16)
            vh = jnp.concatenate([vn_ref[:, sl], pad], axis=0).astype(BF16)
            s = _dot_nt(qh, kh) * (scale * LOG2E) - cq_t[h:h + 1, :]
            s = jnp.where(keep, s, NEG_INF)
            m_o = m_ref[hr, :]
            m_n = jnp.maximum(m_o, jnp.max(s, axis=1, keepdims=True))
            al = jnp.exp2(m_o - m_n)
            pn = jnp.exp2(s - m_n)
            l_n = al * l_ref[hr, :] + jnp.sum(pn, axis=1, keepdims=True)
            o_ref[:, sl] = (al * acc_ref[hr, :] + _dot(pn.astype(BF16), vh)) / l_n


def _fox_sample(q, k_all, v_all, logf_t, cache_k, cache_v, cache_lft, page_table, layer, n_seq, n_new):
    aw = q.shape[1]
    n_heads = aw // HEAD_DIM
    assert n_heads & (n_heads - 1) == 0
    n_pages = page_table.shape[1]
    n_pg = 16 if n_pages % 16 == 0 else (4 if n_pages % 4 == 0 else 1)
    n_steps = n_pages // n_pg

    def page_map(r):
        def f(b, s, pt):
            return (layer, pt[b, (n_steps - 1 - s) * n_pg + r], 0, 0)
        return f

    in_specs = [
        pl.BlockSpec((n_new, aw), lambda b, s, pt: (b, 0)),
        pl.BlockSpec((None, n_new, aw), lambda b, s, pt: (layer, b, 0)),
        pl.BlockSpec((None, n_new, aw), lambda b, s, pt: (layer, b, 0)),
        pl.BlockSpec((None, n_heads, LANES), lambda b, s, pt: (b, 0, 0)),
    ]
    in_specs += [pl.BlockSpec((None, None, n_heads, PAGE_SIZE), page_map(r)) for r in range(n_pg)]
    in_specs += [pl.BlockSpec((None, None, PAGE_SIZE * n_heads, HEAD_DIM), page_map(r))
                 for r in range(n_pg)]
    in_specs += [pl.BlockSpec((None, None, PAGE_SIZE * n_heads, HEAD_DIM), page_map(r))
                 for r in range(n_pg)]
    grid_spec = pltpu.PrefetchScalarGridSpec(
        num_scalar_prefetch=1,
        grid=(n_seq, n_steps),
        in_specs=in_specs,
        out_specs=pl.BlockSpec((n_new, aw), lambda b, s, pt: (b, 0)),
        scratch_shapes=[pltpu.VMEM((n_heads * n_new, LANES), F32),
                        pltpu.VMEM((n_heads * n_new, LANES), F32),
                        pltpu.VMEM((n_heads * n_new, HEAD_DIM), F32),
                        pltpu.VMEM((n_heads, LANES), F32),
                        pltpu.VMEM((n_heads * n_new, n_pg * PAGE_SIZE), F32)],
    )
    args = [page_table, q, k_all, v_all, logf_t] + [cache_lft] * n_pg + [cache_k] * n_pg + [cache_v] * n_pg
    return pl.pallas_call(
        functools.partial(_fox_sample_kernel, n_heads=n_heads, n_pg=n_pg, scale=HEAD_DIM ** -0.5),
        grid_spec=grid_spec,
        out_shape=jax.ShapeDtypeStruct((n_seq * n_new, aw), F32),
        compiler_params=_cparams("parallel", "arbitrary"),
        name="fox_sample",
    )(*args)


def _cmul(a1, a2, x):
    return a1 * x + a2 * pltpu.roll(x, STATE_N, axis=1)


def _chunk_scan(x, ap, n_chunks):
    row = lax.broadcasted_iota(jnp.int32, x.shape, 0)
    hs = x
    sh, lvl = 1, 0
    while sh < n_chunks:
        prev = jnp.where(row >= sh, pltpu.roll(hs, sh, axis=0), 0.0)
        hs = hs + _cmul(ap[2 * lvl:2 * lvl + 1, :], ap[2 * lvl + 1:2 * lvl + 2, :], prev)
        sh *= 2
        lvl += 1
    return hs, jnp.where(row >= 1, pltpu.roll(hs, 1, axis=0), 0.0)


def _ssm_prompt_kernel(s_ref, ws_ref, wo_ref, r_ref, dsk_ref, ap_ref, z_ref, hf_ref,
                       lhs_ref, hb_ref, *, chunk, n_chunks):
    for i in range(chunk):
        lhs_ref[:, i * LANES:(i + 1) * LANES] = s_ref[pl.ds(i, n_chunks, stride=chunk), :].astype(BF16)
    x = _dot(lhs_ref[...], ws_ref[...])
    for g in range(SLAB_GROUPS):
        gs = slice(g * 2 * STATE_N, (g + 1) * 2 * STATE_N)
        hs, hb = _chunk_scan(x[:, gs], ap_ref[g], n_chunks)
        hb_ref[:, gs] = hb.astype(BF16)
        hf_ref[g:g + 1, :] = hs[n_chunks - 1:n_chunks, :]
    hbv = hb_ref[...]
    dsk = dsk_ref[...]
    for i in range(chunk):
        rows = pl.ds(i, n_chunks, stride=chunk)
        y = (_dot(lhs_ref[:, :(i + 1) * LANES], r_ref[(chunk - 1 - i) * LANES:, :])
             + _dot(hbv, wo_ref[i]) + s_ref[rows, :] * dsk)
        z_ref[rows, :] = _gelu_tanh(y)


def _ssm_prompt(s, prm, layer, n_seq, seq_len, chunk):
    sw = s.shape[1]
    n_slabs = sw // LANES
    n_chunks = seq_len // chunk
    ws, wo, rk, dsk, ap = prm
    wq = lambda q, b: (layer, q, 0, 0)
    return pl.pallas_call(
        functools.partial(_ssm_prompt_kernel, chunk=chunk, n_chunks=n_chunks),
        grid=(n_slabs, n_seq),
        in_specs=[
            pl.BlockSpec((seq_len, LANES), lambda q, b: (b, q)),
            pl.BlockSpec((None, None) + ws.shape[2:], wq),
            pl.BlockSpec((None, None) + wo.shape[2:], lambda q, b: (layer, q, 0, 0, 0)),
            pl.BlockSpec((None, None) + rk.shape[2:], wq),
            pl.BlockSpec((None, None, 1, LANES), wq),
            pl.BlockSpec((None, SLAB_GROUPS) + ap.shape[2:], wq),
        ],
        out_specs=[pl.BlockSpec((seq_len, LANES), lambda q, b: (b, q)),
                   pl.BlockSpec((None, SLAB_GROUPS, 2 * STATE_N), lambda q, b: (b, q, 0))],
        out_shape=[jax.ShapeDtypeStruct(s.shape, F32),
                   jax.ShapeDtypeStruct((n_seq, sw // GROUP_CH, 2 * STATE_N), F32)],
        scratch_shapes=[pltpu.VMEM((n_chunks, chunk * LANES), BF16),
                        pltpu.VMEM((n_chunks, SLAB_GROUPS * 2 * STATE_N), BF16)],
        compiler_params=_cparams("parallel", "parallel"),
        name="s5_prompt",
    )(s, ws, wo, rk, dsk, ap)


def _ssm_sample_kernel(u_ref, ws_ref, wo_ref, tp_ref, dsk_ref, ap_ref, h0_ref, z_ref, hf_ref, *, gb):
    for g in range(gb):
        u = u_ref[g]
        ub = u.astype(BF16)
        ap = ap_ref[g]
        hb = h0_ref[g]
        hf_ref[g] = _cmul(ap[0:1, :], ap[1:2, :], hb) + _dot(ub, ws_ref[g])
        y = _dot(ub, tp_ref[g]) + _dot(hb.astype(BF16), wo_ref[g]) + u * dsk_ref[g]
        z_ref[g] = _gelu_tanh(y)


def _ssm_sample(uf, prm, layer, h0):
    n_groups, rows, lc = uf.shape
    gb = 8 if n_groups % 8 == 0 else 1
    g3 = lambda i: (i, 0, 0)
    g4 = lambda i: (layer, i, 0, 0)
    return pl.pallas_call(
        functools.partial(_ssm_sample_kernel, gb=gb),
        grid=(n_groups // gb,),
        in_specs=([pl.BlockSpec((gb, rows, lc), g3)]
                  + [pl.BlockSpec((None, gb) + a.shape[2:], g4) for a in prm]
                  + [pl.BlockSpec((gb, rows, 2 * STATE_N), g3)]),
        out_specs=[pl.BlockSpec((gb, rows, lc), g3),
                   pl.BlockSpec((gb, rows, 2 * STATE_N), g3)],
        out_shape=[jax.ShapeDtypeStruct((n_groups, rows, lc), F32),
                   jax.ShapeDtypeStruct((n_groups, rows, 2 * STATE_N), F32)],
        compiler_params=_cparams("parallel"),
        name="s5_sample",
    )(uf, *prm, h0)


def _ssm_params(lam_re, lam_im, log_dt, b_re, b_im, c_re, c_im, d, chunk, n_chunks):
    hp = dict(precision=HIGHEST)
    dt = jnp.exp(log_dt)[:, None]
    mag = jnp.exp(lam_re * dt)
    a_re = mag * jnp.cos(lam_im * dt)
    a_im = mag * jnp.sin(lam_im * dt)
    den = lam_re * lam_re + lam_im * lam_im
    z_re = ((a_re - 1) * lam_re + a_im * lam_im) / den
    z_im = (a_im * lam_re - (a_re - 1) * lam_im) / den
    bb_re = z_re[..., None] * b_re - z_im[..., None] * b_im
    bb_im = z_re[..., None] * b_im + z_im[..., None] * b_re
    pr, pi = [jnp.ones_like(a_re)], [jnp.zeros_like(a_im)]
    for _ in range(chunk):
        pr.append(pr[-1] * a_re - pi[-1] * a_im)
        pi.append(pr[-2] * a_im + pi[-1] * a_re)
    pw_re = jnp.stack(pr)
    pw_im = jnp.stack(pi)
    dec_re = pw_re[chunk - 1::-1][:chunk]
    dec_im = pw_im[chunk - 1::-1][:chunk]
    ws_re = jnp.einsum('jgn,gnc->gjcn', dec_re, bb_re) - jnp.einsum('jgn,gnc->gjcn', dec_im, bb_im)
    ws_im = jnp.einsum('jgn,gnc->gjcn', dec_re, bb_im) + jnp.einsum('jgn,gnc->gjcn', dec_im, bb_re)
    ws = jnp.concatenate([ws_re, ws_im], axis=-1)
    ca_re = c_re[None] * pw_re[1:, :, None, :] - c_im[None] * pw_im[1:, :, None, :]
    ca_im = c_re[None] * pw_im[1:, :, None, :] + c_im[None] * pw_re[1:, :, None, :]
    wo = jnp.concatenate([ca_re, -ca_im], axis=-1).transpose(1, 3, 0, 2)
    k0_re = c_re[None] * pw_re[:chunk, :, None, :] - c_im[None] * pw_im[:chunk, :, None, :]
    k0_im = c_re[None] * pw_im[:chunk, :, None, :] + c_im[None] * pw_re[:chunk, :, None, :]
    km = (jnp.einsum('mgcn,gnd->gmcd', k0_re, bb_re, **hp)
          - jnp.einsum('mgcn,gnd->gmcd', k0_im, bb_im, **hp))
    lag = np.arange(chunk)[None, :] - np.arange(chunk)[:, None]
    tp = km[:, np.clip(lag, 0, chunk - 1)]
    tp = jnp.where((lag >= 0)[None, :, :, None, None], tp, 0.0).transpose(0, 1, 4, 2, 3)
    qr, qi = pw_re[chunk], pw_im[chunk]
    rows = []
    sh = 1
    while True:
        rows.append(jnp.concatenate([qr, qr], axis=-1))
        rows.append(jnp.concatenate([-qi, qi], axis=-1))
        sh *= 2
        if sh >= n_chunks:
            break
        qr, qi = qr * qr - qi * qi, 2 * qr * qi
    ap = jnp.stack(rows, axis=1)
    pad = (-ap.shape[1]) % 8
    if pad:
        ap = jnp.concatenate([ap, jnp.zeros((ap.shape[0], pad, 2 * STATE_N), F32)], axis=1)
    return ws, wo, tp, d, ap, km


def _group_form(prm, chunk):
    ws, wo, tp, d, ap, _ = prm
    g = ws.shape[0]
    lc = chunk * GROUP_CH
    dsk = jnp.tile(d[:, None, :], (1, chunk, 1)).reshape(g, 1, lc)
    return (ws.reshape(g, lc, 2 * STATE_N).astype(BF16), wo.reshape(g, 2 * STATE_N, lc).astype(BF16),
            tp.reshape(g, lc, lc).astype(BF16), dsk, ap)


def _slab_form(prm, chunk):
    ws, wo, _, d, ap, km = prm
    q = ws.shape[0] // SLAB_GROUPS
    sg, c, n2 = SLAB_GROUPS, GROUP_CH, 2 * STATE_N
    lane_g = np.arange(LANES) // c
    ws_r = ws.reshape(q, sg, chunk, c, n2).transpose(0, 2, 1, 3, 4).reshape(q, chunk * LANES, n2)
    row_g = np.tile(lane_g, chunk)
    ws_s = jnp.concatenate([jnp.where((row_g == h)[None, :, None], ws_r, 0.0) for h in range(sg)],
                           axis=-1).astype(BF16)
    wo_r = wo.reshape(q, sg, n2, chunk, c).transpose(0, 3, 1, 2, 4)
    wo_s = jnp.stack([jnp.pad(wo_r[:, :, g], ((0, 0), (0, 0), (0, 0), (g * c, LANES - (g + 1) * c)))
                      for g in range(sg)], axis=2)
    wo_s = wo_s.reshape(q, chunk, sg * n2, LANES).astype(BF16)
    kt = km.reshape(q, sg, chunk, c, c).transpose(0, 2, 1, 4, 3).reshape(q, chunk, LANES, c)
    blk = jnp.where((lane_g[:, None] == lane_g[None, :])[None, None], jnp.tile(kt, (1, 1, 1, sg)), 0.0)
    rk = blk[:, ::-1].reshape(q, chunk * LANES, LANES).astype(BF16)
    return ws_s, wo_s, rk, d.reshape(q, 1, LANES), ap


def _mix_kernel(h_ref, a_ref, z_ref, ga_ref, gb_ref, wp_ref, wa_ref, wb_ref, wo_ref, *rest):
    o_ref = rest[-1]
    o_att = _dot(a_ref[...].astype(BF16), wp_ref[...])
    zb = z_ref[...].astype(BF16)
    o_ssm = _dot(zb, wa_ref[...]) * jax.nn.sigmoid(_dot(zb, wb_ref[...]))
    g = (ga_ref[...] * o_att + gb_ref[...] * o_ssm).astype(BF16)
    o_ref[...] = h_ref[...] + _dot(g, wo_ref[...])


def _mix(h, a, z, ga, gb, wp, wa, wb, wo, layer, row0, tm, out_buf=None):
    m, d = h.shape
    n_rows = a.shape[0]
    rb0 = row0 // tm
    row = lambda w: pl.BlockSpec((tm, w), lambda i: (i, 0))
    mrow = pl.BlockSpec((tm, d), lambda i: (rb0 + i, 0))
    in_specs = [mrow, row(a.shape[1]), row(z.shape[1]), row(d), row(d),
                _layer_resident(wp, layer), _layer_resident(wa, layer),
                _layer_resident(wb, layer), _layer_resident(wo, layer)]
    args = [h, a, z, ga, gb, wp, wa, wb, wo]
    aliases = {}
    if out_buf is not None:
        aliases[len(args)] = 0
        in_specs.append(pl.BlockSpec(memory_space=pl.ANY))
        args.append(out_buf)
    return pl.pallas_call(
        _mix_kernel,
        grid=(n_rows // tm,),
        in_specs=in_specs,
        out_specs=mrow,
        out_shape=jax.ShapeDtypeStruct((m, d), F32),
        input_output_aliases=aliases,
        compiler_params=_cparams("parallel"),
        name="mix",
    )(*args)


def _xattn_kernel(h_ref, g_ref, wq_ref, qg_ref, mk_ref, mv_ref, wo_ref, o_ref, *, n_heads, scale):
    x = h_ref[...]
    u = _rms(x, g_ref[...]).astype(BF16)
    qx = _dot(u, wq_ref[...])
    outs = []
    for h in range(n_heads):
        sl = slice(h * X_HEAD_DIM, (h + 1) * X_HEAD_DIM)
        qh = _rms(qx[:, sl], qg_ref[...]).astype(BF16)
        s = _dot_nt(qh, mk_ref[:, sl].astype(BF16)) * scale
        e = jnp.exp(s - jnp.max(s, axis=1, keepdims=True))
        p = e / jnp.sum(e, axis=1, keepdims=True)
        outs.append(_dot(p.astype(BF16), mv_ref[:, sl].astype(BF16)))
    o = jnp.concatenate(outs, axis=1).astype(BF16)
    o_ref[...] = x + _dot(o, wo_ref[...])


def _xattn(h, g, wq, qg, mk_all, mv_all, wo, layer, n_seq, seq_len, row0, tm):
    m, d = h.shape
    xw = wq.shape[2]
    n_mem = mk_all.shape[1] // n_seq
    nt = seq_len // tm
    blk0 = row0 // tm
    rows = lambda b, i: (blk0 + b * nt + i, 0)
    small = lambda shape: pl.BlockSpec(shape, lambda b, i: (0,) * len(shape), pipeline_mode=pl.Buffered(1))
    return pl.pallas_call(
        functools.partial(_xattn_kernel, n_heads=xw // X_HEAD_DIM, scale=X_HEAD_DIM ** -0.5),
        grid=(n_seq, nt),
        in_specs=[pl.BlockSpec((tm, d), rows),
                  small((1, d)), _layer_resident(wq, layer), small((1, X_HEAD_DIM)),
                  pl.BlockSpec((None, n_mem, xw), lambda b, i: (layer, b, 0)),
                  pl.BlockSpec((None, n_mem, xw), lambda b, i: (layer, b, 0)),
                  _layer_resident(wo, layer)],
        out_specs=pl.BlockSpec((tm, d), rows),
        out_shape=jax.ShapeDtypeStruct((m, d), F32),
        input_output_aliases={0: 0},
        compiler_params=_cparams("parallel", "arbitrary"),
        name="xattn",
    )(h, g.reshape(1, d), wq, qg.reshape(1, X_HEAD_DIM), mk_all, mv_all, wo)


def kernel(x_prompt, x_sample, cache_k, cache_v, cache_logf, state_ssm_re, state_ssm_im, cache_mem_k, cache_mem_v, page_table, mem_prompt, ffn1_norm, ffn1_w_gate, ffn1_w_up, ffn1_w_down, mix_norm, w_in, b_forget, q_norm, k_norm, ssm_lambda_re, ssm_lambda_im, ssm_log_dt, ssm_b_re, ssm_b_im, ssm_c_re, ssm_c_im, ssm_d, ssm_glu_w, ssm_glu_v, w_att_proj, w_out, cross_norm, mem_norm, w_cq, w_ck, w_cv, cq_norm, ck_norm, w_co, ffn2_norm, ffn2_w_gate, ffn2_w_up, ffn2_w_down):
    bp, t_p, d = x_prompt.shape
    bd, t_s, _ = x_sample.shape
    depth = ffn1_norm.shape[0]
    aw = w_att_proj.shape[1]
    n_heads = aw // HEAD_DIM
    sw = ssm_glu_w.shape[1]
    n_groups = sw // GROUP_CH
    xw = w_cq.shape[2]
    x_heads = xw // X_HEAD_DIM
    n_mem = mem_prompt.shape[1]
    n_pool = cache_k.shape[1]
    mp, ms = bp * t_p, bd * t_s
    mt = mp + ms
    chunk_p = 16
    nc_p = t_p // chunk_p
    tm_p = _row_tile(t_p, 1024, LANES)
    tm_mix = _row_tile(mp, 256)
    assert n_heads <= 8 and t_s == 8 and t_p % chunk_p == 0 and mp % ms == 0 and sw % LANES == 0

    x = jnp.concatenate([x_prompt.reshape(mp, d), x_sample.reshape(ms, d)], axis=0)
    mem = mem_prompt.reshape(bp * n_mem, d)
    ck = cache_k.reshape(depth, n_pool, PAGE_SIZE * n_heads, HEAD_DIM)
    cv = cache_v.reshape(depth, n_pool, PAGE_SIZE * n_heads, HEAD_DIM)
    clft = cache_logf.transpose(0, 1, 3, 2)
    cmk = cache_mem_k.reshape(depth, bd * n_mem, xw)
    cmv = cache_mem_v.reshape(depth, bd * n_mem, xw)

    o_f = 3 * aw
    o_s = o_f + n_heads
    bf = lambda w: w.astype(BF16)
    w_qkv = bf(w_in[:, :, :o_f])
    w_f = bf(jnp.pad(w_in[:, :, o_f:o_s], ((0, 0), (0, 0), (0, LANES - n_heads))))
    w_sg = bf(w_in[:, :, o_s:])
    b_f = jnp.pad(b_forget, ((0, 0), (0, LANES - n_heads)))
    f1g, f1u, f1d = bf(ffn1_w_gate), bf(ffn1_w_up), bf(ffn1_w_down)
    f2g, f2u, f2d = bf(ffn2_w_gate), bf(ffn2_w_up), bf(ffn2_w_down)
    wp_b, wa_b, wb_b, wo_b = bf(w_att_proj), bf(ssm_glu_w), bf(ssm_glu_v), bf(w_out)
    wcq_b, wck_b, wcv_b, wco_b = bf(w_cq), bf(w_ck), bf(w_cv), bf(w_co)

    ssm_raw = (ssm_lambda_re, ssm_lambda_im, ssm_log_dt, ssm_b_re, ssm_b_im, ssm_c_re, ssm_c_im, ssm_d)
    prm_p_all = jax.vmap(lambda *a: _slab_form(_ssm_params(*a, chunk_p, nc_p), chunk_p))(*ssm_raw)
    prm_s_all = jax.vmap(lambda *a: _group_form(_ssm_params(*a, t_s, 1), t_s))(*ssm_raw)

    kp = vp = ks = vs = mkp = mvp = None
    lf_p_l, lf_s_l, hf_p_l, hf_s_l = [], [], [], []
    for l in range(depth):
        first = l == 0
        st = lambda buf: depth if first else buf

        h1, u = _ffn(x, ffn1_norm[l], f1g, f1u, f1d, l, g_next=mix_norm[l])

        qg = q_norm[l].reshape(1, HEAD_DIM)
        kg = k_norm[l].reshape(1, HEAD_DIM)
        pr = dict(row0=0, n_rows=mp, tm=tm_p)
        sr = dict(row0=mp, n_rows=ms, tm=ms)
        (q_p,) = _proj(u, w_qkv, l, 0, aw, mode="headnorm", aux=qg, outs=((BF16, None),), **pr)
        (q_s,) = _proj(u, w_qkv, l, 0, aw, mode="headnorm", aux=qg, **sr)
        (kp,) = _proj(u, w_qkv, l, aw, aw, mode="headnorm", aux=kg, outs=((F32, st(kp)),), **pr)
        (ks,) = _proj(u, w_qkv, l, aw, aw, mode="headnorm", aux=kg, outs=((F32, st(ks)),), **sr)
        (vp,) = _proj(u, w_qkv, l, 2 * aw, aw, outs=((F32, st(vp)),), **pr)
        (vs,) = _proj(u, w_qkv, l, 2 * aw, aw, outs=((F32, st(vs)),), **sr)
        (lf_p,) = _proj(u, w_f, l, 0, LANES, mode="logsigmoid", aux=b_f[l:l + 1], **pr)
        (lf_s,) = _proj(u, w_f, l, 0, LANES, mode="logsigmoid", aux=b_f[l:l + 1], **sr)
        (s_p,) = _proj(u, w_sg, l, 0, sw, **pr)
        (s_s,) = _proj(u, w_sg, l, 0, sw, **sr)
        (ga_p,) = _proj(u, w_sg, l, sw, d, mode="sigmoid", **pr)
        (ga_s,) = _proj(u, w_sg, l, sw, d, mode="sigmoid", **sr)
        (gb_p,) = _proj(u, w_sg, l, sw + d, d, mode="sigmoid", **pr)
        (gb_s,) = _proj(u, w_sg, l, sw + d, d, mode="sigmoid", **sr)

        ct = _cumsum(lf_p, bp, t_p)
        att_p = _fox_prompt(q_p, kp, vp, ct, l, bp, t_p)
        lf_st = lf_s.reshape(bd, t_s, LANES)[:, :, :n_heads].transpose(0, 2, 1)
        lf_st = jnp.pad(lf_st, ((0, 0), (0, 0), (0, LANES - t_s)))
        att_s = _fox_sample(q_s, ks, vs, lf_st, ck, cv, clft, page_table, l, bd, t_s)

        z_p, hf_p = _ssm_prompt(s_p, prm_p_all, l, bp, t_p, chunk_p)
        uf_s = (s_s.reshape(bd, t_s, n_groups, GROUP_CH).transpose(2, 0, 1, 3)
                .reshape(n_groups, bd, t_s * GROUP_CH))
        h0 = jnp.concatenate([state_ssm_re[l], state_ssm_im[l]], axis=-1).transpose(1, 0, 2)
        zf_s, hf_s = _ssm_sample(uf_s, prm_s_all, l, h0)
        z_s = zf_s.reshape(n_groups, bd, t_s, GROUP_CH).transpose(1, 2, 0, 3).reshape(ms, sw)

        h2 = _mix(h1, att_p, z_p, ga_p, gb_p, wp_b, wa_b, wb_b, wo_b, l, 0, tm_mix)
        h2 = _mix(h1, att_s, z_s, ga_s, gb_s, wp_b, wa_b, wb_b, wo_b, l, mp, ms, out_buf=h2)

        mr = dict(row0=0, n_rows=bp * n_mem, tm=_row_tile(bp * n_mem, 512), gain_in=mem_norm[l])
        (mkp,) = _proj(mem, wck_b, l, 0, xw, mode="headnorm", aux=ck_norm[l].reshape(1, X_HEAD_DIM),
                       outs=((F32, st(mkp)),), **mr)
        (mvp,) = _proj(mem, wcv_b, l, 0, xw, outs=((F32, st(mvp)),), **mr)
        h3 = _xattn(h2, cross_norm[l], wcq_b, cq_norm[l], mkp, mvp, wco_b, l, bp, t_p, 0,
                    _row_tile(t_p, 512, LANES))
        h3 = _xattn(h3, cross_norm[l], wcq_b, cq_norm[l], cmk, cmv, wco_b, l, bd, t_s, mp, t_s)

        x = _ffn(h3, ffn2_norm[l], f2g, f2u, f2d, l)

        lf_p_l.append(lf_p)
        lf_s_l.append(lf_s)
        hf_p_l.append(hf_p)
        hf_s_l.append(hf_s)

    lf_p_all = jnp.stack(lf_p_l)[:, :, :n_heads]
    lf_s_all = jnp.stack(lf_s_l)[:, :, :n_heads]
    hf_p_all = jnp.stack(hf_p_l)
    hf_s_all = jnp.stack(hf_s_l).transpose(0, 2, 1, 3)
    return (x[:mp].reshape(bp, t_p, d), x[mp:].reshape(bd, t_s, d),
            kp.reshape(depth, bp, t_p, n_heads, HEAD_DIM), vp.reshape(depth, bp, t_p, n_heads, HEAD_DIM),
            lf_p_all.reshape(depth, bp, t_p, n_heads),
            hf_p_all[..., :STATE_N], hf_p_all[..., STATE_N:],
            mkp.reshape(depth, bp, n_mem, x_heads, X_HEAD_DIM), mvp.reshape(depth, bp, n_mem, x_heads, X_HEAD_DIM),
            ks.reshape(depth, bd, t_s, n_heads, HEAD_DIM), vs.reshape(depth, bd, t_s, n_heads, HEAD_DIM),
            lf_s_all.reshape(depth, bd, t_s, n_heads),
            hf_s_all[..., :STATE_N], hf_s_all[..., STATE_N:])
```

```python
import functools
import math

import jax
import jax.numpy as jnp
import numpy as np
from jax import lax
from jax.experimental import pallas as pl
from jax.experimental.pallas import tpu as pltpu

F32 = jnp.float32
BF16 = jnp.bfloat16
EPS = 1e-6
LANES = 128
HEAD_DIM = 128
GROUP_CH = 16
STATE_N = 64
PAGE_SIZE = 128
X_HEAD_DIM = 128
SLAB_GROUPS = LANES // GROUP_CH
NEG_INF = float("-inf")
M_INIT = -1e30
LOG2E = math.log2(math.e)
VMEM_LIMIT = 56 * 1024 * 1024
HIGHEST = lax.Precision.HIGHEST


def _cparams(*sem):
    return pltpu.CompilerParams(dimension_semantics=sem, vmem_limit_bytes=VMEM_LIMIT)


def _layer_resident(w, layer):
    return pl.BlockSpec((None,) + w.shape[1:], lambda *_: (layer, 0, 0), pipeline_mode=pl.Buffered(1))


def _row_tile(m, target, mult=16):
    best = None
    for t in range(mult, min(m, target) + 1, mult):
        if m % t == 0:
            best = t
    assert best is not None, (m, target)
    return best


def _rms(x, g):
    return (x * lax.rsqrt(jnp.mean(x * x, axis=-1, keepdims=True) + EPS)) * g


def _dot(a, b):
    return jnp.dot(a, b, preferred_element_type=F32)


def _dot_nt(a, b):
    return lax.dot_general(a, b, (((1,), (1,)), ((), ())), preferred_element_type=F32)


def _dot_exact(a, b):
    return jnp.dot(a, b, preferred_element_type=F32, precision=HIGHEST)


def _log_sigmoid(x):
    return -(jnp.maximum(-x, 0.0) + jnp.log1p(jnp.exp(-jnp.abs(x))))


def _gelu_tanh(x):
    return 0.5 * x * (1.0 + jnp.tanh(math.sqrt(2.0 / math.pi) * (x + 0.044715 * (x * x * x))))


def _ffn_kernel(x_ref, g_ref, wg_ref, wu_ref, wd_ref, *rest, n_ff, with_norm_out):
    if with_norm_out:
        g2_ref, o_ref, u_ref, h_ref = rest
    else:
        o_ref, h_ref = rest
    j = pl.program_id(1)

    @pl.when(j == 0)
    def _():
        h_ref[...] = _rms(x_ref[...], g_ref[...]).astype(BF16)

    h = h_ref[...]
    gate = _dot(h, wg_ref[...])
    up = _dot(h, wu_ref[...])
    act = (gate * jax.nn.sigmoid(gate) * up).astype(BF16)
    part = _dot(act, wd_ref[...])

    @pl.when(j == 0)
    def _():
        o_ref[...] = part

    @pl.when(j > 0)
    def _():
        o_ref[...] += part

    @pl.when(j == n_ff - 1)
    def _():
        y = x_ref[...] + 0.5 * o_ref[...]
        o_ref[...] = y
        if with_norm_out:
            u_ref[...] = _rms(y, g2_ref[...]).astype(BF16)


def _ffn(x, g, wg, wu, wd, layer, g_next=None):
    m, d = x.shape
    ff = wg.shape[2]
    tm = _row_tile(m, 704)
    tf = _row_tile(ff, 512, LANES)
    n_ff = ff // tf
    with_norm_out = g_next is not None
    in_specs = [
        pl.BlockSpec((tm, d), lambda i, j: (i, 0)),
        pl.BlockSpec((1, d), lambda i, j: (0, 0)),
        pl.BlockSpec((None, d, tf), lambda i, j: (layer, 0, j)),
        pl.BlockSpec((None, d, tf), lambda i, j: (layer, 0, j)),
        pl.BlockSpec((None, tf, d), lambda i, j: (layer, j, 0)),
    ]
    args = [x, g.reshape(1, d), wg, wu, wd]
    out_shape = [jax.ShapeDtypeStruct((m, d), F32)]
    out_specs = [pl.BlockSpec((tm, d), lambda i, j: (i, 0))]
    if with_norm_out:
        in_specs.append(pl.BlockSpec((1, d), lambda i, j: (0, 0)))
        args.append(g_next.reshape(1, d))
        out_shape.append(jax.ShapeDtypeStruct((m, d), BF16))
        out_specs.append(pl.BlockSpec((tm, d), lambda i, j: (i, 0)))
    res = pl.pallas_call(
        functools.partial(_ffn_kernel, n_ff=n_ff, with_norm_out=with_norm_out),
        grid=(m // tm, n_ff),
        in_specs=in_specs,
        out_specs=out_specs,
        out_shape=out_shape,
        scratch_shapes=[pltpu.VMEM((tm, d), BF16)],
        compiler_params=_cparams("parallel", "arbitrary"),
        name="ffn",
    )(*args)
    return res if with_norm_out else res[0]


def _head_norm(r, g):
    cols = []
    for h in range(r.shape[1] // HEAD_DIM):
        cols.append(_rms(r[:, h * HEAD_DIM:(h + 1) * HEAD_DIM], g))
    return cols[0] if len(cols) == 1 else jnp.concatenate(cols, axis=1)


def _proj_kernel(u_ref, w_ref, aux_ref, *rest, mode, norm_in, n_out):
    if norm_in:
        gin_ref, rest = rest[0], rest[1:]
        u = _rms(u_ref[...], gin_ref[...]).astype(BF16)
    else:
        u = u_ref[...]
    outs = rest[len(rest) - n_out:]
    r = _dot(u, w_ref[...])
    if mode == "headnorm":
        r = _head_norm(r, aux_ref[...])
    elif mode == "sigmoid":
        r = jax.nn.sigmoid(r)
    elif mode == "logsigmoid":
        r = _log_sigmoid(r + aux_ref[...])
    for o in outs:
        o[...] = r.astype(o.dtype)


def _proj(u, w, layer, col0, n, row0, n_rows, tm, mode="plain", aux=None, outs=((F32, None),),
          gain_in=None):
    kdim = u.shape[1]
    tn = _row_tile(math.gcd(n, col0), 1024, LANES)
    assert row0 % tm == 0 and n_rows % tm == 0
    rb0, cb0 = row0 // tm, col0 // tn
    if aux is None:
        aux = jnp.zeros((1, LANES), F32)
    in_specs = [
        pl.BlockSpec((tm, kdim), lambda j, i: (rb0 + i, 0)),
        pl.BlockSpec((None, kdim, tn), lambda j, i: (layer, 0, cb0 + j)),
        pl.BlockSpec(aux.shape, lambda j, i: (0, 0)),
    ]
    args = [u, w, aux]
    if gain_in is not None:
        in_specs.append(pl.BlockSpec((1, kdim), lambda j, i: (0, 0)))
        args.append(gain_in.reshape(1, kdim))
    out_specs, out_shape, aliases = [], [], {}
    for k, (dt, stacked) in enumerate(outs):
        if stacked is None:
            out_specs.append(pl.BlockSpec((tm, tn), lambda j, i: (i, j)))
            out_shape.append(jax.ShapeDtypeStruct((n_rows, n), dt))
        else:
            out_specs.append(pl.BlockSpec((None, tm, tn), lambda j, i: (layer, i, j)))
            if isinstance(stacked, int):
                out_shape.append(jax.ShapeDtypeStruct((stacked, n_rows, n), dt))
            else:
                out_shape.append(jax.ShapeDtypeStruct(stacked.shape, dt))
                aliases[len(args)] = k
                in_specs.append(pl.BlockSpec(memory_space=pl.ANY))
                args.append(stacked)
    return pl.pallas_call(
        functools.partial(_proj_kernel, mode=mode, norm_in=gain_in is not None, n_out=len(outs)),
        grid=(n // tn, n_rows // tm),
        in_specs=in_specs,
        out_specs=out_specs,
        out_shape=out_shape,
        input_output_aliases=aliases,
        compiler_params=_cparams("parallel", "parallel"),
        name="proj_" + mode,
    )(*args)


def _cumsum_kernel(x_ref, ct_ref, carry_ref):
    @pl.when(pl.program_id(1) == 0)
    def _():
        carry_ref[...] = jnp.zeros_like(carry_ref)

    x = x_ref[...]
    tc = x.shape[0]
    tri = (lax.broadcasted_iota(jnp.int32, (tc, tc), 1)
           <= lax.broadcasted_iota(jnp.int32, (tc, tc), 0)).astype(F32)
    c = _dot_exact(tri, x) + carry_ref[...]
    ct_ref[...] = c.T[:8, :]
    carry_ref[...] = c[tc - 1:tc, :]


def _cumsum(logf, n_seq, seq_len):
    tc = _row_tile(seq_len, 512, LANES)
    nc = seq_len // tc
    return pl.pallas_call(
        _cumsum_kernel,
        grid=(n_seq, nc),
        in_specs=[pl.BlockSpec((tc, LANES), lambda b, c: (b * nc + c, 0))],
        out_specs=pl.BlockSpec((None, 8, tc), lambda b, c: (b, 0, c)),
        out_shape=jax.ShapeDtypeStruct((n_seq, 8, seq_len), F32),
        scratch_shapes=[pltpu.VMEM((1, LANES), F32)],
        compiler_params=_cparams("parallel", "arbitrary"),
        name="logf_cumsum",
    )(logf)


def _fox_prompt_kernel(qi_ref, kj_ref, q_ref, k_ref, v_ref, ct_ref, o_ref,
                       m_ref, l_ref, acc_ref, kb_ref, vb_ref, cj_ref, s_ref, p_ref, al_ref,
                       *, n_heads, scale, rc):
    t = pl.program_id(1)
    qi = qi_ref[t]
    kj = kj_ref[t]
    tq = q_ref.shape[0]
    tk = k_ref.shape[0]
    rep = tk // LANES

    @pl.when(kj == 0)
    def _():
        m_ref[...] = jnp.full(m_ref.shape, M_INIT, F32)
        l_ref[...] = jnp.zeros_like(l_ref)
        acc_ref[...] = jnp.zeros_like(acc_ref)

    kb_ref[...] = k_ref[...].astype(BF16)
    vb_ref[...] = v_ref[...].astype(BF16)
    for h in range(n_heads):
        cj_ref[h] = jnp.broadcast_to(ct_ref[h:h + 1, :] * LOG2E, (8, tk))

    def step(masked):
        if masked:
            diff = (lax.broadcasted_iota(jnp.int32, (rc, tk), 1)
                    - lax.broadcasted_iota(jnp.int32, (rc, tk), 0))
        for h in range(n_heads):
            sl = slice(h * HEAD_DIM, (h + 1) * HEAD_DIM)
            s_ref[...] = _dot_nt(q_ref[:, sl], kb_ref[:, sl])
            cj = pltpu.repeat(cj_ref[h], rc // 8, axis=0)
            for c in range(tq // rc):
                rows = slice(c * rc, (c + 1) * rc)
                s = s_ref[rows, :] * (scale * LOG2E) - cj
                if masked:
                    s = jnp.where(diff <= c * rc, s, NEG_INF)
                m_old = m_ref[h, rows, :]
                m_new = jnp.maximum(m_old, jnp.max(s, axis=1, keepdims=True))
                alpha = jnp.exp2(m_old - m_new)
                p = jnp.exp2(s - pltpu.repeat(m_new, rep, axis=1))
                l_ref[h, rows, :] = alpha * l_ref[h, rows, :] + jnp.sum(p, axis=1, keepdims=True)
                m_ref[h, rows, :] = m_new
                al_ref[rows, :] = alpha
                p_ref[rows, :] = p.astype(BF16)
            acc_ref[h] = al_ref[...] * acc_ref[h] + _dot(p_ref[...], vb_ref[:, sl])

    @pl.when(kj < qi)
    def _():
        step(False)

    @pl.when(kj == qi)
    def _():
        step(True)
        for h in range(n_heads):
            o_ref[:, h * HEAD_DIM:(h + 1) * HEAD_DIM] = (acc_ref[h] / l_ref[h]).astype(o_ref.dtype)


def _fox_prompt(q, k_all, v_all, ct, layer, n_seq, seq_len):
    aw = q.shape[1]
    n_heads = aw // HEAD_DIM
    tq = _row_tile(seq_len, 512, LANES)
    nq = seq_len // tq
    pairs = [(i, j) for i in range(nq) for j in range(i + 1)]
    qi_tab = jnp.asarray([p[0] for p in pairs], jnp.int32)
    kj_tab = jnp.asarray([p[1] for p in pairs], jnp.int32)
    grid_spec = pltpu.PrefetchScalarGridSpec(
        num_scalar_prefetch=2,
        grid=(n_seq, len(pairs)),
        in_specs=[
            pl.BlockSpec((tq, aw), lambda b, t, qi, kj: (b * nq + qi[t], 0)),
            pl.BlockSpec((None, tq, aw), lambda b, t, qi, kj: (layer, b * nq + kj[t], 0)),
            pl.BlockSpec((None, tq, aw), lambda b, t, qi, kj: (layer, b * nq + kj[t], 0)),
            pl.BlockSpec((None, 8, tq), lambda b, t, qi, kj: (b, 0, kj[t])),
        ],
        out_specs=pl.BlockSpec((tq, aw), lambda b, t, qi, kj: (b * nq + qi[t], 0)),
        scratch_shapes=[pltpu.VMEM((n_heads, tq, LANES), F32),
                        pltpu.VMEM((n_heads, tq, LANES), F32),
                        pltpu.VMEM((n_heads, tq, HEAD_DIM), F32),
                        pltpu.VMEM((tq, aw), BF16),
                        pltpu.VMEM((tq, aw), BF16),
                        pltpu.VMEM((n_heads, 8, tq), F32),
                        pltpu.VMEM((tq, tq), F32),
                        pltpu.VMEM((tq, tq), BF16),
                        pltpu.VMEM((tq, LANES), F32)],
    )
    return pl.pallas_call(
        functools.partial(_fox_prompt_kernel, n_heads=n_heads, scale=HEAD_DIM ** -0.5, rc=min(64, tq)),
        grid_spec=grid_spec,
        out_shape=jax.ShapeDtypeStruct((n_seq * seq_len, aw), BF16),
        compiler_params=_cparams("parallel", "arbitrary"),
        name="fox_prompt",
    )(qi_tab, kj_tab, q, k_all, v_all, ct)


def _fox_sample_kernel(pt_ref, q_ref, kn_ref, vn_ref, lfnt_ref, *rest, n_heads, n_pg, scale):
    lft_refs = rest[:n_pg]
    k_refs = rest[n_pg:2 * n_pg]
    v_refs = rest[2 * n_pg:3 * n_pg]
    o_ref, m_ref, l_ref, acc_ref, carry_ref, s_ref = rest[3 * n_pg:]
    del pt_ref
    s_idx = pl.program_id(1)
    n_steps = pl.num_programs(1)
    tn = q_ref.shape[0]
    rows = n_pg * n_heads

    @pl.when(s_idx == 0)
    def _():
        m_ref[...] = jnp.full(m_ref.shape, M_INIT, F32)
        l_ref[...] = jnp.zeros_like(l_ref)
        acc_ref[...] = jnp.zeros_like(acc_ref)
        carry_ref[...] = jnp.zeros_like(carry_ref)

    lft = jnp.concatenate([lft_refs[r][...] for r in range(n_pg)], axis=0)
    later = (lax.broadcasted_iota(jnp.int32, (PAGE_SIZE, PAGE_SIZE), 0)
             > lax.broadcasted_iota(jnp.int32, (PAGE_SIZE, PAGE_SIZE), 1)).astype(F32)
    page_sum = _dot_exact(lft, jnp.ones((PAGE_SIZE, LANES), F32))
    ra = lax.broadcasted_iota(jnp.int32, (rows, rows), 0)
    rb = lax.broadcasted_iota(jnp.int32, (rows, rows), 1)
    same_head = (ra & (n_heads - 1)) == (rb & (n_heads - 1))
    later_pages = jnp.where(same_head & (rb > ra), 1.0, 0.0)
    carry = carry_ref[...]
    carry_rows = jnp.concatenate([carry] * n_pg, axis=0)
    rev = (_dot_exact(lft, later) + _dot_exact(later_pages, page_sum) + carry_rows) * LOG2E
    ha = lax.broadcasted_iota(jnp.int32, (n_heads, rows), 0)
    hb = lax.broadcasted_iota(jnp.int32, (n_heads, rows), 1)
    per_head = jnp.where(ha == (hb & (n_heads - 1)), 1.0, 0.0)
    carry_ref[...] = carry + _dot_exact(per_head, page_sum)

    for h in range(n_heads):
        sl = slice(h * HEAD_DIM, (h + 1) * HEAD_DIM)
        qh = q_ref[:, sl].astype(BF16)
        for r in range(n_pg):
            kh = k_refs[r][pl.ds(h, PAGE_SIZE, stride=n_heads), :].astype(BF16)
            row = r * n_heads + h
            s_ref[h * tn:(h + 1) * tn, r * PAGE_SIZE:(r + 1) * PAGE_SIZE] = (
                _dot_nt(qh, kh) * (scale * LOG2E) + rev[row:row + 1, :])
    sc = s_ref[...]
    m_old = m_ref[...]
    m_new = jnp.maximum(m_old, jnp.max(sc, axis=1, keepdims=True))
    alpha = jnp.exp2(m_old - m_new)
    p = jnp.exp2(sc - pltpu.repeat(m_new, n_pg, axis=1))
    l_ref[...] = alpha * l_ref[...] + jnp.sum(p, axis=1, keepdims=True)
    m_ref[...] = m_new
    s_ref[...] = p
    for h in range(n_heads):
        hr = slice(h * tn, (h + 1) * tn)
        acc = alpha[hr, :] * acc_ref[hr, :]
        for r in range(n_pg):
            vh = v_refs[r][pl.ds(h, PAGE_SIZE, stride=n_heads), :].astype(BF16)
            acc = acc + _dot(s_ref[hr, r * PAGE_SIZE:(r + 1) * PAGE_SIZE].astype(BF16), vh)
        acc_ref[hr, :] = acc

    @pl.when(s_idx == n_steps - 1)
    def _():
        upto = (lax.broadcasted_iota(jnp.int32, (LANES, LANES), 0)
                <= lax.broadcasted_iota(jnp.int32, (LANES, LANES), 1)).astype(F32)
        cq_t = _dot_exact(lfnt_ref[...], upto) * LOG2E
        keep = (lax.broadcasted_iota(jnp.int32, (tn, LANES), 1)
                <= lax.broadcasted_iota(jnp.int32, (tn, LANES), 0))
        pad = jnp.zeros((LANES - tn, HEAD_DIM), F32)
        for h in range(n_heads):
            sl = slice(h * HEAD_DIM, (h + 1) * HEAD_DIM)
            hr = slice(h * tn, (h + 1) * tn)
            qh = q_ref[:, sl].astype(BF16)
            kh = jnp.concatenate([kn_ref[:, sl], pad], axis=0).astype(BF16)
            vh = jnp.concatenate([vn_ref[:, sl], pad], axis=0).astype(BF16)
            s = _dot_nt(qh, kh) * (scale * LOG2E) - cq_t[h:h + 1, :]
            s = jnp.where(keep, s, NEG_INF)
            m_o = m_ref[hr, :]
            m_n = jnp.maximum(m_o, jnp.max(s, axis=1, keepdims=True))
            al = jnp.exp2(m_o - m_n)
            pn = jnp.exp2(s - m_n)
            l_n = al * l_ref[hr, :] + jnp.sum(pn, axis=1, keepdims=True)
            o_ref[:, sl] = (al * acc_ref[hr, :] + _dot(pn.astype(BF16), vh)) / l_n


def _fox_sample(q, k_all, v_all, logf_t, cache_k, cache_v, cache_lft, page_table, layer, n_seq, n_new):
    aw = q.shape[1]
    n_heads = aw // HEAD_DIM
    assert n_heads & (n_heads - 1) == 0
    n_pages = page_table.shape[1]
    n_pg = 16 if n_pages % 16 == 0 else (4 if n_pages % 4 == 0 else 1)
    n_steps = n_pages // n_pg

    def page_map(r):
        def f(b, s, pt):
            return (layer, pt[b, (n_steps - 1 - s) * n_pg + r], 0, 0)
        return f

    in_specs = [
        pl.BlockSpec((n_new, aw), lambda b, s, pt: (b, 0)),
        pl.BlockSpec((None, n_new, aw), lambda b, s, pt: (layer, b, 0)),
        pl.BlockSpec((None, n_new, aw), lambda b, s, pt: (layer, b, 0)),
        pl.BlockSpec((None, n_heads, LANES), lambda b, s, pt: (b, 0, 0)),
    ]
    in_specs += [pl.BlockSpec((None, None, n_heads, PAGE_SIZE), page_map(r)) for r in range(n_pg)]
    in_specs += [pl.BlockSpec((None, None, PAGE_SIZE * n_heads, HEAD_DIM), page_map(r))
                 for r in range(n_pg)]
    in_specs += [pl.BlockSpec((None, None, PAGE_SIZE * n_heads, HEAD_DIM), page_map(r))
                 for r in range(n_pg)]
    grid_spec = pltpu.PrefetchScalarGridSpec(
        num_scalar_prefetch=1,
        grid=(n_seq, n_steps),
        in_specs=in_specs,
        out_specs=pl.BlockSpec((n_new, aw), lambda b, s, pt: (b, 0)),
        scratch_shapes=[pltpu.VMEM((n_heads * n_new, LANES), F32),
                        pltpu.VMEM((n_heads * n_new, LANES), F32),
                        pltpu.VMEM((n_heads * n_new, HEAD_DIM), F32),
                        pltpu.VMEM((n_heads, LANES), F32),
                        pltpu.VMEM((n_heads * n_new, n_pg * PAGE_SIZE), F32)],
    )
    args = [page_table, q, k_all, v_all, logf_t] + [cache_lft] * n_pg + [cache_k] * n_pg + [cache_v] * n_pg
    return pl.pallas_call(
        functools.partial(_fox_sample_kernel, n_heads=n_heads, n_pg=n_pg, scale=HEAD_DIM ** -0.5),
        grid_spec=grid_spec,
        out_shape=jax.ShapeDtypeStruct((n_seq * n_new, aw), F32),
        compiler_params=_cparams("parallel", "arbitrary"),
        name="fox_sample",
    )(*args)


def _cmul(a1, a2, x):
    return a1 * x + a2 * pltpu.roll(x, STATE_N, axis=1)


def _chunk_scan(x, ap, n_chunks):
    row = lax.broadcasted_iota(jnp.int32, x.shape, 0)
    hs = x
    sh, lvl = 1, 0
    while sh < n_chunks:
        prev = jnp.where(row >= sh, pltpu.roll(hs, sh, axis=0), 0.0)
        hs = hs + _cmul(ap[2 * lvl:2 * lvl + 1, :], ap[2 * lvl + 1:2 * lvl + 2, :], prev)
        sh *= 2
        lvl += 1
    return hs, jnp.where(row >= 1, pltpu.roll(hs, 1, axis=0), 0.0)


def _ssm_prompt_kernel(s_ref, ws_ref, wo_ref, r_ref, dsk_ref, ap_ref, z_ref, hf_ref,
                       lhs_ref, hb_ref, *, chunk, n_chunks):
    for i in range(chunk):
        lhs_ref[:, i * LANES:(i + 1) * LANES] = s_ref[pl.ds(i, n_chunks, stride=chunk), :].astype(BF16)
    x = _dot(lhs_ref[...], ws_ref[...])
    for g in range(SLAB_GROUPS):
        gs = slice(g * 2 * STATE_N, (g + 1) * 2 * STATE_N)
        hs, hb = _chunk_scan(x[:, gs], ap_ref[g], n_chunks)
        hb_ref[:, gs] = hb.astype(BF16)
        hf_ref[g:g + 1, :] = hs[n_chunks - 1:n_chunks, :]
    hbv = hb_ref[...]
    dsk = dsk_ref[...]
    for i in range(chunk):
        rows = pl.ds(i, n_chunks, stride=chunk)
        y = (_dot(lhs_ref[:, :(i + 1) * LANES], r_ref[(chunk - 1 - i) * LANES:, :])
             + _dot(hbv, wo_ref[i]) + s_ref[rows, :] * dsk)
        z_ref[rows, :] = _gelu_tanh(y)


def _ssm_prompt(s, prm, layer, n_seq, seq_len, chunk):
    sw = s.shape[1]
    n_slabs = sw // LANES
    n_chunks = seq_len // chunk
    ws, wo, rk, dsk, ap = prm
    wq = lambda q, b: (layer, q, 0, 0)
    return pl.pallas_call(
        functools.partial(_ssm_prompt_kernel, chunk=chunk, n_chunks=n_chunks),
        grid=(n_slabs, n_seq),
        in_specs=[
            pl.BlockSpec((seq_len, LANES), lambda q, b: (b, q)),
            pl.BlockSpec((None, None) + ws.shape[2:], wq),
            pl.BlockSpec((None, None) + wo.shape[2:], lambda q, b: (layer, q, 0, 0, 0)),
            pl.BlockSpec((None, None) + rk.shape[2:], wq),
            pl.BlockSpec((None, None, 1, LANES), wq),
            pl.BlockSpec((None, SLAB_GROUPS) + ap.shape[2:], wq),
        ],
        out_specs=[pl.BlockSpec((seq_len, LANES), lambda q, b: (b, q)),
                   pl.BlockSpec((None, SLAB_GROUPS, 2 * STATE_N), lambda q, b: (b, q, 0))],
        out_shape=[jax.ShapeDtypeStruct(s.shape, F32),
                   jax.ShapeDtypeStruct((n_seq, sw // GROUP_CH, 2 * STATE_N), F32)],
        scratch_shapes=[pltpu.VMEM((n_chunks, chunk * LANES), BF16),
                        pltpu.VMEM((n_chunks, SLAB_GROUPS * 2 * STATE_N), BF16)],
        compiler_params=_cparams("parallel", "parallel"),
        name="s5_prompt",
    )(s, ws, wo, rk, dsk, ap)


def _ssm_sample_kernel(u_ref, ws_ref, wo_ref, tp_ref, dsk_ref, ap_ref, h0_ref, z_ref, hf_ref, *, gb):
    for g in range(gb):
        u = u_ref[g]
        ub = u.astype(BF16)
        ap = ap_ref[g]
        hb = h0_ref[g]
        hf_ref[g] = _cmul(ap[0:1, :], ap[1:2, :], hb) + _dot(ub, ws_ref[g])
        y = _dot(ub, tp_ref[g]) + _dot(hb.astype(BF16), wo_ref[g]) + u * dsk_ref[g]
        z_ref[g] = _gelu_tanh(y)


def _ssm_sample(uf, prm, layer, h0):
    n_groups, rows, lc = uf.shape
    gb = 8 if n_groups % 8 == 0 else 1
    g3 = lambda i: (i, 0, 0)
    g4 = lambda i: (layer, i, 0, 0)
    return pl.pallas_call(
        functools.partial(_ssm_sample_kernel, gb=gb),
        grid=(n_groups // gb,),
        in_specs=([pl.BlockSpec((gb, rows, lc), g3)]
                  + [pl.BlockSpec((None, gb) + a.shape[2:], g4) for a in prm]
                  + [pl.BlockSpec((gb, rows, 2 * STATE_N), g3)]),
        out_specs=[pl.BlockSpec((gb, rows, lc), g3),
                   pl.BlockSpec((gb, rows, 2 * STATE_N), g3)],
        out_shape=[jax.ShapeDtypeStruct((n_groups, rows, lc), F32),
                   jax.ShapeDtypeStruct((n_groups, rows, 2 * STATE_N), F32)],
        compiler_params=_cparams("parallel"),
        name="s5_sample",
    )(uf, *prm, h0)


def _ssm_params(lam_re, lam_im, log_dt, b_re, b_im, c_re, c_im, d, chunk, n_chunks):
    hp = dict(precision=HIGHEST)
    dt = jnp.exp(log_dt)[:, None]
    mag = jnp.exp(lam_re * dt)
    a_re = mag * jnp.cos(lam_im * dt)
    a_im = mag * jnp.sin(lam_im * dt)
    den = lam_re * lam_re + lam_im * lam_im
    z_re = ((a_re - 1) * lam_re + a_im * lam_im) / den
    z_im = (a_im * lam_re - (a_re - 1) * lam_im) / den
    bb_re = z_re[..., None] * b_re - z_im[..., None] * b_im
    bb_im = z_re[..., None] * b_im + z_im[..., None] * b_re
    pr, pi = [jnp.ones_like(a_re)], [jnp.zeros_like(a_im)]
    for _ in range(chunk):
        pr.append(pr[-1] * a_re - pi[-1] * a_im)
        pi.append(pr[-2] * a_im + pi[-1] * a_re)
    pw_re = jnp.stack(pr)
    pw_im = jnp.stack(pi)
    dec_re = pw_re[chunk - 1::-1][:chunk]
    dec_im = pw_im[chunk - 1::-1][:chunk]
    ws_re = jnp.einsum('jgn,gnc->gjcn', dec_re, bb_re) - jnp.einsum('jgn,gnc->gjcn', dec_im, bb_im)
    ws_im = jnp.einsum('jgn,gnc->gjcn', dec_re, bb_im) + jnp.einsum('jgn,gnc->gjcn', dec_im, bb_re)
    ws = jnp.concatenate([ws_re, ws_im], axis=-1)
    ca_re = c_re[None] * pw_re[1:, :, None, :] - c_im[None] * pw_im[1:, :, None, :]
    ca_im = c_re[None] * pw_im[1:, :, None, :] + c_im[None] * pw_re[1:, :, None, :]
    wo = jnp.concatenate([ca_re, -ca_im], axis=-1).transpose(1, 3, 0, 2)
    k0_re = c_re[None] * pw_re[:chunk, :, None, :] - c_im[None] * pw_im[:chunk, :, None, :]
    k0_im = c_re[None] * pw_im[:chunk, :, None, :] + c_im[None] * pw_re[:chunk, :, None, :]
    km = (jnp.einsum('mgcn,gnd->gmcd', k0_re, bb_re, **hp)
          - jnp.einsum('mgcn,gnd->gmcd', k0_im, bb_im, **hp))
    lag = np.arange(chunk)[None, :] - np.arange(chunk)[:, None]
    tp = km[:, np.clip(lag, 0, chunk - 1)]
    tp = jnp.where((lag >= 0)[None, :, :, None, None], tp, 0.0).transpose(0, 1, 4, 2, 3)
    qr, qi = pw_re[chunk], pw_im[chunk]
    rows = []
    sh = 1
    while True:
        rows.append(jnp.concatenate([qr, qr], axis=-1))
        rows.append(jnp.concatenate([-qi, qi], axis=-1))
        sh *= 2
        if sh >= n_chunks:
            break
        qr, qi = qr * qr - qi * qi, 2 * qr * qi
    ap = jnp.stack(rows, axis=1)
    pad = (-ap.shape[1]) % 8
    if pad:
        ap = jnp.concatenate([ap, jnp.zeros((ap.shape[0], pad, 2 * STATE_N), F32)], axis=1)
    return ws, wo, tp, d, ap, km


def _group_form(prm, chunk):
    ws, wo, tp, d, ap, _ = prm
    g = ws.shape[0]
    lc = chunk * GROUP_CH
    dsk = jnp.tile(d[:, None, :], (1, chunk, 1)).reshape(g, 1, lc)
    return (ws.reshape(g, lc, 2 * STATE_N).astype(BF16), wo.reshape(g, 2 * STATE_N, lc).astype(BF16),
            tp.reshape(g, lc, lc).astype(BF16), dsk, ap)


def _slab_form(prm, chunk):
    ws, wo, _, d, ap, km = prm
    q = ws.shape[0] // SLAB_GROUPS
    sg, c, n2 = SLAB_GROUPS, GROUP_CH, 2 * STATE_N
    lane_g = np.arange(LANES) // c
    ws_r = ws.reshape(q, sg, chunk, c, n2).transpose(0, 2, 1, 3, 4).reshape(q, chunk * LANES, n2)
    row_g = np.tile(lane_g, chunk)
    ws_s = jnp.concatenate([jnp.where((row_g == h)[None, :, None], ws_r, 0.0) for h in range(sg)],
                           axis=-1).astype(BF16)
    wo_r = wo.reshape(q, sg, n2, chunk, c).transpose(0, 3, 1, 2, 4)
    wo_s = jnp.stack([jnp.pad(wo_r[:, :, g], ((0, 0), (0, 0), (0, 0), (g * c, LANES - (g + 1) * c)))
                      for g in range(sg)], axis=2)
    wo_s = wo_s.reshape(q, chunk, sg * n2, LANES).astype(BF16)
    kt = km.reshape(q, sg, chunk, c, c).transpose(0, 2, 1, 4, 3).reshape(q, chunk, LANES, c)
    blk = jnp.where((lane_g[:, None] == lane_g[None, :])[None, None], jnp.tile(kt, (1, 1, 1, sg)), 0.0)
    rk = blk[:, ::-1].reshape(q, chunk * LANES, LANES).astype(BF16)
    return ws_s, wo_s, rk, d.reshape(q, 1, LANES), ap


def _mix_kernel(h_ref, a_ref, z_ref, ga_ref, gb_ref, wp_ref, wa_ref, wb_ref, wo_ref, *rest):
    o_ref = rest[-1]
    o_att = _dot(a_ref[...].astype(BF16), wp_ref[...])
    zb = z_ref[...].astype(BF16)
    o_ssm = _dot(zb, wa_ref[...]) * jax.nn.sigmoid(_dot(zb, wb_ref[...]))
    g = (ga_ref[...] * o_att + gb_ref[...] * o_ssm).astype(BF16)
    o_ref[...] = h_ref[...] + _dot(g, wo_ref[...])


def _mix(h, a, z, ga, gb, wp, wa, wb, wo, layer, row0, tm, out_buf=None):
    m, d = h.shape
    n_rows = a.shape[0]
    rb0 = row0 // tm
    row = lambda w: pl.BlockSpec((tm, w), lambda i: (i, 0))
    mrow = pl.BlockSpec((tm, d), lambda i: (rb0 + i, 0))
    in_specs = [mrow, row(a.shape[1]), row(z.shape[1]), row(d), row(d),
                _layer_resident(wp, layer), _layer_resident(wa, layer),
                _layer_resident(wb, layer), _layer_resident(wo, layer)]
    args = [h, a, z, ga, gb, wp, wa, wb, wo]
    aliases = {}
    if out_buf is not None:
        aliases[len(args)] = 0
        in_specs.append(pl.BlockSpec(memory_space=pl.ANY))
        args.append(out_buf)
    return pl.pallas_call(
        _mix_kernel,
        grid=(n_rows // tm,),
        in_specs=in_specs,
        out_specs=mrow,
        out_shape=jax.ShapeDtypeStruct((m, d), F32),
        input_output_aliases=aliases,
        compiler_params=_cparams("parallel"),
        name="mix",
    )(*args)


def _xattn_kernel(h_ref, g_ref, wq_ref, qg_ref, mk_ref, mv_ref, wo_ref, o_ref, *, n_heads, scale):
    x = h_ref[...]
    u = _rms(x, g_ref[...]).astype(BF16)
    qx = _dot(u, wq_ref[...])
    outs = []
    for h in range(n_heads):
        sl = slice(h * X_HEAD_DIM, (h + 1) * X_HEAD_DIM)
        qh = _rms(qx[:, sl], qg_ref[...]).astype(BF16)
        s = _dot_nt(qh, mk_ref[:, sl].astype(BF16)) * scale
        e = jnp.exp(s - jnp.max(s, axis=1, keepdims=True))
        p = e / jnp.sum(e, axis=1, keepdims=True)
        outs.append(_dot(p.astype(BF16), mv_ref[:, sl].astype(BF16)))
    o = jnp.concatenate(outs, axis=1).astype(BF16)
    o_ref[...] = x + _dot(o, wo_ref[...])


def _xattn(h, g, wq, qg, mk_all, mv_all, wo, layer, n_seq, seq_len, row0, tm):
    m, d = h.shape
    xw = wq.shape[2]
    n_mem = mk_all.shape[1] // n_seq
    nt = seq_len // tm
    blk0 = row0 // tm
    rows = lambda b, i: (blk0 + b * nt + i, 0)
    small = lambda shape: pl.BlockSpec(shape, lambda b, i: (0,) * len(shape), pipeline_mode=pl.Buffered(1))
    return pl.pallas_call(
        functools.partial(_xattn_kernel, n_heads=xw // X_HEAD_DIM, scale=X_HEAD_DIM ** -0.5),
        grid=(n_seq, nt),
        in_specs=[pl.BlockSpec((tm, d), rows),
                  small((1, d)), _layer_resident(wq, layer), small((1, X_HEAD_DIM)),
                  pl.BlockSpec((None, n_mem, xw), lambda b, i: (layer, b, 0)),
                  pl.BlockSpec((None, n_mem, xw), lambda b, i: (layer, b, 0)),
                  _layer_resident(wo, layer)],
        out_specs=pl.BlockSpec((tm, d), rows),
        out_shape=jax.ShapeDtypeStruct((m, d), F32),
        input_output_aliases={0: 0},
        compiler_params=_cparams("parallel", "arbitrary"),
        name="xattn",
    )(h, g.reshape(1, d), wq, qg.reshape(1, X_HEAD_DIM), mk_all, mv_all, wo)


def kernel(x_prompt, x_sample, cache_k, cache_v, cache_logf, state_ssm_re, state_ssm_im, cache_mem_k, cache_mem_v, page_table, mem_prompt, ffn1_norm, ffn1_w_gate, ffn1_w_up, ffn1_w_down, mix_norm, w_in, b_forget, q_norm, k_norm, ssm_lambda_re, ssm_lambda_im, ssm_log_dt, ssm_b_re, ssm_b_im, ssm_c_re, ssm_c_im, ssm_d, ssm_glu_w, ssm_glu_v, w_att_proj, w_out, cross_norm, mem_norm, w_cq, w_ck, w_cv, cq_norm, ck_norm, w_co, ffn2_norm, ffn2_w_gate, ffn2_w_up, ffn2_w_down):
    bp, t_p, d = x_prompt.shape
    bd, t_s, _ = x_sample.shape
    depth = ffn1_norm.shape[0]
    aw = w_att_proj.shape[1]
    n_heads = aw // HEAD_DIM
    sw = ssm_glu_w.shape[1]
    n_groups = sw // GROUP_CH
    xw = w_cq.shape[2]
    x_heads = xw // X_HEAD_DIM
    n_mem = mem_prompt.shape[1]
    n_pool = cache_k.shape[1]
    mp, ms = bp * t_p, bd * t_s
    mt = mp + ms
    chunk_p = 16
    nc_p = t_p // chunk_p
    tm_p = _row_tile(t_p, 2048, LANES)
    tm_mix = _row_tile(mp, 256)
    assert n_heads <= 8 and t_s == 8 and t_p % chunk_p == 0 and mp % ms == 0 and sw % LANES == 0

    x = jnp.concatenate([x_prompt.reshape(mp, d), x_sample.reshape(ms, d)], axis=0)
    mem = mem_prompt.reshape(bp * n_mem, d)
    ck = cache_k.reshape(depth, n_pool, PAGE_SIZE * n_heads, HEAD_DIM)
    cv = cache_v.reshape(depth, n_pool, PAGE_SIZE * n_heads, HEAD_DIM)
    clft = cache_logf.transpose(0, 1, 3, 2)
    cmk = cache_mem_k.reshape(depth, bd * n_mem, xw)
    cmv = cache_mem_v.reshape(depth, bd * n_mem, xw)

    o_f = 3 * aw
    o_s = o_f + n_heads
    bf = lambda w: w.astype(BF16)
    w_qkv = bf(w_in)
    w_f = jnp.pad(w_qkv[:, :, o_f:o_s], ((0, 0), (0, 0), (0, LANES - n_heads)))
    w_sg = w_qkv[:, :, o_s:]
    b_f = jnp.pad(b_forget, ((0, 0), (0, LANES - n_heads)))
    f1g, f1u, f1d = bf(ffn1_w_gate), bf(ffn1_w_up), bf(ffn1_w_down)
    f2g, f2u, f2d = bf(ffn2_w_gate), bf(ffn2_w_up), bf(ffn2_w_down)
    wp_b, wa_b, wb_b, wo_b = bf(w_att_proj), bf(ssm_glu_w), bf(ssm_glu_v), bf(w_out)
    wcq_b, wck_b, wcv_b, wco_b = bf(w_cq), bf(w_ck), bf(w_cv), bf(w_co)

    ssm_raw = (ssm_lambda_re, ssm_lambda_im, ssm_log_dt, ssm_b_re, ssm_b_im, ssm_c_re, ssm_c_im, ssm_d)
    prm_p_all = jax.vmap(lambda *a: _slab_form(_ssm_params(*a, chunk_p, nc_p), chunk_p))(*ssm_raw)
    prm_s_all = jax.vmap(lambda *a: _group_form(_ssm_params(*a, t_s, 1), t_s))(*ssm_raw)

    kp = vp = ks = vs = mkp = mvp = None
    lf_p_l, lf_s_l, hf_p_l, hf_s_l = [], [], [], []
    for l in range(depth):
        first = l == 0
        st = lambda buf: depth if first else buf

        h1, u = _ffn(x, ffn1_norm[l], f1g, f1u, f1d, l, g_next=mix_norm[l])

        qg = q_norm[l].reshape(1, HEAD_DIM)
        kg = k_norm[l].reshape(1, HEAD_DIM)
        pr = dict(row0=0, n_rows=mp, tm=tm_p)
        sr = dict(row0=mp, n_rows=ms, tm=ms)
        (q_p,) = _proj(u, w_qkv, l, 0, aw, mode="headnorm", aux=qg, outs=((BF16, None),), **pr)
        (q_s,) = _proj(u, w_qkv, l, 0, aw, mode="headnorm", aux=qg, **sr)
        (kp,) = _proj(u, w_qkv, l, aw, aw, mode="headnorm", aux=kg, outs=((F32, st(kp)),), **pr)
        (ks,) = _proj(u, w_qkv, l, aw, aw, mode="headnorm", aux=kg, outs=((F32, st(ks)),), **sr)
        (vp,) = _proj(u, w_qkv, l, 2 * aw, aw, outs=((F32, st(vp)),), **pr)
        (vs,) = _proj(u, w_qkv, l, 2 * aw, aw, outs=((F32, st(vs)),), **sr)
        (lf_p,) = _proj(u, w_f, l, 0, LANES, mode="logsigmoid", aux=b_f[l:l + 1], **pr)
        (lf_s,) = _proj(u, w_f, l, 0, LANES, mode="logsigmoid", aux=b_f[l:l + 1], **sr)
        (s_p,) = _proj(u, w_sg, l, 0, sw, **pr)
        (s_s,) = _proj(u, w_sg, l, 0, sw, **sr)
        (ga_p,) = _proj(u, w_sg, l, sw, d, mode="sigmoid", **pr)
        (ga_s,) = _proj(u, w_sg, l, sw, d, mode="sigmoid", **sr)
        (gb_p,) = _proj(u, w_sg, l, sw + d, d, mode="sigmoid", **pr)
        (gb_s,) = _proj(u, w_sg, l, sw + d, d, mode="sigmoid", **sr)

        ct = _cumsum(lf_p, bp, t_p)
        att_p = _fox_prompt(q_p, kp, vp, ct, l, bp, t_p)
        lf_st = lf_s.reshape(bd, t_s, LANES)[:, :, :n_heads].transpose(0, 2, 1)
        lf_st = jnp.pad(lf_st, ((0, 0), (0, 0), (0, LANES - t_s)))
        att_s = _fox_sample(q_s, ks, vs, lf_st, ck, cv, clft, page_table, l, bd, t_s)

        z_p, hf_p = _ssm_prompt(s_p, prm_p_all, l, bp, t_p, chunk_p)
        uf_s = (s_s.reshape(bd, t_s, n_groups, GROUP_CH).transpose(2, 0, 1, 3)
                .reshape(n_groups, bd, t_s * GROUP_CH))
        h0 = jnp.concatenate([state_ssm_re[l], state_ssm_im[l]], axis=-1).transpose(1, 0, 2)
        zf_s, hf_s = _ssm_sample(uf_s, prm_s_all, l, h0)
        z_s = zf_s.reshape(n_groups, bd, t_s, GROUP_CH).transpose(1, 2, 0, 3).reshape(ms, sw)

        h2 = _mix(h1, att_p, z_p, ga_p, gb_p, wp_b, wa_b, wb_b, wo_b, l, 0, tm_mix)
        h2 = _mix(h1, att_s, z_s, ga_s, gb_s, wp_b, wa_b, wb_b, wo_b, l, mp, ms, out_buf=h2)

        mr = dict(row0=0, n_rows=bp * n_mem, tm=_row_tile(bp * n_mem, 512), gain_in=mem_norm[l])
        (mkp,) = _proj(mem, wck_b, l, 0, xw, mode="headnorm", aux=ck_norm[l].reshape(1, X_HEAD_DIM),
                       outs=((F32, st(mkp)),), **mr)
        (mvp,) = _proj(mem, wcv_b, l, 0, xw, outs=((F32, st(mvp)),), **mr)
        h3 = _xattn(h2, cross_norm[l], wcq_b, cq_norm[l], mkp, mvp, wco_b, l, bp, t_p, 0,
                    _row_tile(t_p, 512, LANES))
        h3 = _xattn(h3, cross_norm[l], wcq_b, cq_norm[l], cmk, cmv, wco_b, l, bd, t_s, mp, t_s)

        x = _ffn(h3, ffn2_norm[l], f2g, f2u, f2d, l)

        lf_p_l.append(lf_p)
        lf_s_l.append(lf_s)
        hf_p_l.append(hf_p)
        hf_s_l.append(hf_s)

    lf_p_all = jnp.stack(lf_p_l)[:, :, :n_heads]
    lf_s_all = jnp.stack(lf_s_l)[:, :, :n_heads]
    hf_p_all = jnp.stack(hf_p_l)
    hf_s_all = jnp.stack(hf_s_l).transpose(0, 2, 1, 3)
    return (x[:mp].reshape(bp, t_p, d), x[mp:].reshape(bd, t_s, d),
            kp.reshape(depth, bp, t_p, n_heads, HEAD_DIM), vp.reshape(depth, bp, t_p, n_heads, HEAD_DIM),
            lf_p_all.reshape(depth, bp, t_p, n_heads),
            hf_p_all[..., :STATE_N], hf_p_all[..., STATE_N:],
            mkp.reshape(depth, bp, n_mem, x_heads, X_HEAD_DIM), mvp.reshape(depth, bp, n_mem, x_heads, X_HEAD_DIM),
            ks.reshape(depth, bd, t_s, n_heads, HEAD_DIM), vs.reshape(depth, bd, t_s, n_heads, HEAD_DIM),
            lf_s_all.reshape(depth, bd, t_s, n_heads),
            hf_s_all[..., :STATE_N], hf_s_all[..., STATE_N:])
```

```python
import functools
import math

import jax
import jax.numpy as jnp
import numpy as np
from jax import lax
from jax.experimental import pallas as pl
from jax.experimental.pallas import tpu as pltpu

F32 = jnp.float32
BF16 = jnp.bfloat16
EPS = 1e-6
LANES = 128
HEAD_DIM = 128
GROUP_CH = 16
STATE_N = 64
PAGE_SIZE = 128
X_HEAD_DIM = 128
SLAB_GROUPS = LANES // GROUP_CH
NEG_INF = float("-inf")
M_INIT = -1e30
LOG2E = math.log2(math.e)
VMEM_LIMIT = 56 * 1024 * 1024
HIGHEST = lax.Precision.HIGHEST


def _cparams(*sem):
    return pltpu.CompilerParams(dimension_semantics=sem, vmem_limit_bytes=VMEM_LIMIT)


def _layer_resident(w, layer):
    return pl.BlockSpec((None,) + w.shape[1:], lambda *_: (layer, 0, 0), pipeline_mode=pl.Buffered(1))


def _row_tile(m, target, mult=16):
    best = None
    for t in range(mult, min(m, target) + 1, mult):
        if m % t == 0:
            best = t
    assert best is not None, (m, target)
    return best


def _rms(x, g):
    return (x * lax.rsqrt(jnp.mean(x * x, axis=-1, keepdims=True) + EPS)) * g


def _dot(a, b):
    return jnp.dot(a, b, preferred_element_type=F32)


def _dot_nt(a, b):
    return lax.dot_general(a, b, (((1,), (1,)), ((), ())), preferred_element_type=F32)


def _dot_exact(a, b):
    return jnp.dot(a, b, preferred_element_type=F32, precision=HIGHEST)


def _log_sigmoid(x):
    return -(jnp.maximum(-x, 0.0) + jnp.log1p(jnp.exp(-jnp.abs(x))))


def _gelu_tanh(x):
    return 0.5 * x * (1.0 + jnp.tanh(math.sqrt(2.0 / math.pi) * (x + 0.044715 * (x * x * x))))


def _ffn_kernel(x_ref, g_ref, wg_ref, wu_ref, wd_ref, *rest, n_ff, with_norm_out):
    if with_norm_out:
        g2_ref, o_ref, u_ref, h_ref = rest
    else:
        o_ref, h_ref = rest
    j = pl.program_id(1)

    @pl.when(j == 0)
    def _():
        h_ref[...] = _rms(x_ref[...], g_ref[...]).astype(BF16)

    h = h_ref[...]
    gate = _dot(h, wg_ref[...])
    up = _dot(h, wu_ref[...])
    act = (gate * jax.nn.sigmoid(gate) * up).astype(BF16)
    part = _dot(act, wd_ref[...])

    @pl.when(j == 0)
    def _():
        o_ref[...] = part

    @pl.when(j > 0)
    def _():
        o_ref[...] += part

    @pl.when(j == n_ff - 1)
    def _():
        y = x_ref[...] + 0.5 * o_ref[...]
        o_ref[...] = y
        if with_norm_out:
            u_ref[...] = _rms(y, g2_ref[...]).astype(BF16)


def _ffn(x, g, wg, wu, wd, layer, g_next=None):
    m, d = x.shape
    ff = wg.shape[2]
    tm = _row_tile(m, 704)
    tf = _row_tile(ff, 512, LANES)
    n_ff = ff // tf
    with_norm_out = g_next is not None
    in_specs = [
        pl.BlockSpec((tm, d), lambda i, j: (i, 0)),
        pl.BlockSpec((1, d), lambda i, j: (0, 0)),
        pl.BlockSpec((None, d, tf), lambda i, j: (layer, 0, j)),
        pl.BlockSpec((None, d, tf), lambda i, j: (layer, 0, j)),
        pl.BlockSpec((None, tf, d), lambda i, j: (layer, j, 0)),
    ]
    args = [x, g.reshape(1, d), wg, wu, wd]
    out_shape = [jax.ShapeDtypeStruct((m, d), F32)]
    out_specs = [pl.BlockSpec((tm, d), lambda i, j: (i, 0))]
    if with_norm_out:
        in_specs.append(pl.BlockSpec((1, d), lambda i, j: (0, 0)))
        args.append(g_next.reshape(1, d))
        out_shape.append(jax.ShapeDtypeStruct((m, d), BF16))
        out_specs.append(pl.BlockSpec((tm, d), lambda i, j: (i, 0)))
    res = pl.pallas_call(
        functools.partial(_ffn_kernel, n_ff=n_ff, with_norm_out=with_norm_out),
        grid=(m // tm, n_ff),
        in_specs=in_specs,
        out_specs=out_specs,
        out_shape=out_shape,
        scratch_shapes=[pltpu.VMEM((tm, d), BF16)],
        compiler_params=_cparams("parallel", "arbitrary"),
        name="ffn",
    )(*args)
    return res if with_norm_out else res[0]


def _head_norm(r, g):
    cols = []
    for h in range(r.shape[1] // HEAD_DIM):
        cols.append(_rms(r[:, h * HEAD_DIM:(h + 1) * HEAD_DIM], g))
    return cols[0] if len(cols) == 1 else jnp.concatenate(cols, axis=1)


def _proj_kernel(u_ref, w_ref, aux_ref, *rest, mode, norm_in, n_out):
    if norm_in:
        gin_ref, rest = rest[0], rest[1:]
        u = _rms(u_ref[...], gin_ref[...]).astype(BF16)
    else:
        u = u_ref[...]
    outs = rest[len(rest) - n_out:]
    r = _dot(u, w_ref[...])
    if mode == "headnorm":
        r = _head_norm(r, aux_ref[...])
    elif mode == "sigmoid":
        r = jax.nn.sigmoid(r)
    elif mode == "logsigmoid":
        r = _log_sigmoid(r + aux_ref[...])
    for o in outs:
        o[...] = r.astype(o.dtype)


def _proj(u, w, layer, col0, n, row0, n_rows, tm, mode="plain", aux=None, outs=((F32, None),),
          gain_in=None):
    kdim = u.shape[1]
    tn = _row_tile(math.gcd(n, col0), 1024, LANES)
    assert row0 % tm == 0 and n_rows % tm == 0
    rb0, cb0 = row0 // tm, col0 // tn
    if aux is None:
        aux = jnp.zeros((1, LANES), F32)
    in_specs = [
        pl.BlockSpec((tm, kdim), lambda j, i: (rb0 + i, 0)),
        pl.BlockSpec((None, kdim, tn), lambda j, i: (layer, 0, cb0 + j)),
        pl.BlockSpec(aux.shape, lambda j, i: (0, 0)),
    ]
    args = [u, w, aux]
    if gain_in is not None:
        in_specs.append(pl.BlockSpec((1, kdim), lambda j, i: (0, 0)))
        args.append(gain_in.reshape(1, kdim))
    out_specs, out_shape, aliases = [], [], {}
    for k, (dt, stacked) in enumerate(outs):
        if stacked is None:
            out_specs.append(pl.BlockSpec((tm, tn), lambda j, i: (i, j)))
            out_shape.append(jax.ShapeDtypeStruct((n_rows, n), dt))
        else:
            out_specs.append(pl.BlockSpec((None, tm, tn), lambda j, i: (layer, i, j)))
            if isinstance(stacked, int):
                out_shape.append(jax.ShapeDtypeStruct((stacked, n_rows, n), dt))
            else:
                out_shape.append(jax.ShapeDtypeStruct(stacked.shape, dt))
                aliases[len(args)] = k
                in_specs.append(pl.BlockSpec(memory_space=pl.ANY))
                args.append(stacked)
    return pl.pallas_call(
        functools.partial(_proj_kernel, mode=mode, norm_in=gain_in is not None, n_out=len(outs)),
        grid=(n // tn, n_rows // tm),
        in_specs=in_specs,
        out_specs=out_specs,
        out_shape=out_shape,
        input_output_aliases=aliases,
        compiler_params=_cparams("parallel", "parallel"),
        name="proj_" + mode,
    )(*args)


def _cumsum_kernel(x_ref, ct_ref, carry_ref):
    @pl.when(pl.program_id(1) == 0)
    def _():
        carry_ref[...] = jnp.zeros_like(carry_ref)

    x = x_ref[...]
    tc = x.shape[0]
    tri = (lax.broadcasted_iota(jnp.int32, (tc, tc), 1)
           <= lax.broadcasted_iota(jnp.int32, (tc, tc), 0)).astype(F32)
    c = _dot_exact(tri, x) + carry_ref[...]
    ct_ref[...] = c.T[:8, :]
    carry_ref[...] = c[tc - 1:tc, :]


def _cumsum(logf, n_seq, seq_len):
    tc = _row_tile(seq_len, 512, LANES)
    nc = seq_len // tc
    return pl.pallas_call(
        _cumsum_kernel,
        grid=(n_seq, nc),
        in_specs=[pl.BlockSpec((tc, LANES), lambda b, c: (b * nc + c, 0))],
        out_specs=pl.BlockSpec((None, 8, tc), lambda b, c: (b, 0, c)),
        out_shape=jax.ShapeDtypeStruct((n_seq, 8, seq_len), F32),
        scratch_shapes=[pltpu.VMEM((1, LANES), F32)],
        compiler_params=_cparams("parallel", "arbitrary"),
        name="logf_cumsum",
    )(logf)


def _fox_prompt_kernel(qi_ref, kj_ref, q_ref, k_ref, v_ref, ct_ref, o_ref,
                       m_ref, l_ref, acc_ref, kb_ref, vb_ref, cj_ref, s_ref, p_ref, al_ref,
                       *, n_heads, scale, rc):
    t = pl.program_id(1)
    qi = qi_ref[t]
    kj = kj_ref[t]
    tq = q_ref.shape[0]
    tk = k_ref.shape[0]
    rep = tk // LANES

    @pl.when(kj == 0)
    def _():
        m_ref[...] = jnp.full(m_ref.shape, M_INIT, F32)
        l_ref[...] = jnp.zeros_like(l_ref)
        acc_ref[...] = jnp.zeros_like(acc_ref)

    kb_ref[...] = k_ref[...].astype(BF16)
    vb_ref[...] = v_ref[...].astype(BF16)
    for h in range(n_heads):
        cj_ref[h] = jnp.broadcast_to(ct_ref[h:h + 1, :] * LOG2E, (8, tk))

    def step(masked):
        if masked:
            diff = (lax.broadcasted_iota(jnp.int32, (rc, tk), 1)
                    - lax.broadcasted_iota(jnp.int32, (rc, tk), 0))
        for h in range(n_heads):
            sl = slice(h * HEAD_DIM, (h + 1) * HEAD_DIM)
            s_ref[...] = _dot_nt(q_ref[:, sl], kb_ref[:, sl])
            cj = pltpu.repeat(cj_ref[h], rc // 8, axis=0)
            for c in range(tq // rc):
                rows = slice(c * rc, (c + 1) * rc)
                s = s_ref[rows, :] * (scale * LOG2E) - cj
                if masked:
                    s = jnp.where(diff <= c * rc, s, NEG_INF)
                m_old = m_ref[h, rows, :]
                m_new = jnp.maximum(m_old, jnp.max(s, axis=1, keepdims=True))
                alpha = jnp.exp2(m_old - m_new)
                p = jnp.exp2(s - pltpu.repeat(m_new, rep, axis=1))
                l_ref[h, rows, :] = alpha * l_ref[h, rows, :] + jnp.sum(p, axis=1, keepdims=True)
                m_ref[h, rows, :] = m_new
                al_ref[rows, :] = alpha
                p_ref[rows, :] = p.astype(BF16)
            acc_ref[h] = al_ref[...] * acc_ref[h] + _dot(p_ref[...], vb_ref[:, sl])

    @pl.when(kj < qi)
    def _():
        step(False)

    @pl.when(kj == qi)
    def _():
        step(True)
        for h in range(n_heads):
            o_ref[:, h * HEAD_DIM:(h + 1) * HEAD_DIM] = (acc_ref[h] / l_ref[h]).astype(o_ref.dtype)


def _fox_prompt(q, k_all, v_all, ct, layer, n_seq, seq_len):
    aw = q.shape[1]
    n_heads = aw // HEAD_DIM
    tq = _row_tile(seq_len, 512, LANES)
    nq = seq_len // tq
    pairs = [(i, j) for i in range(nq) for j in range(i + 1)]
    qi_tab = jnp.asarray([p[0] for p in pairs], jnp.int32)
    kj_tab = jnp.asarray([p[1] for p in pairs], jnp.int32)
    grid_spec = pltpu.PrefetchScalarGridSpec(
        num_scalar_prefetch=2,
        grid=(n_seq, len(pairs)),
        in_specs=[
            pl.BlockSpec((tq, aw), lambda b, t, qi, kj: (b * nq + qi[t], 0)),
            pl.BlockSpec((None, tq, aw), lambda b, t, qi, kj: (layer, b * nq + kj[t], 0)),
            pl.BlockSpec((None, tq, aw), lambda b, t, qi, kj: (layer, b * nq + kj[t], 0)),
            pl.BlockSpec((None, 8, tq), lambda b, t, qi, kj: (b, 0, kj[t])),
        ],
        out_specs=pl.BlockSpec((tq, aw), lambda b, t, qi, kj: (b * nq + qi[t], 0)),
        scratch_shapes=[pltpu.VMEM((n_heads, tq, LANES), F32),
                        pltpu.VMEM((n_heads, tq, LANES), F32),
                        pltpu.VMEM((n_heads, tq, HEAD_DIM), F32),
                        pltpu.VMEM((tq, aw), BF16),
                        pltpu.VMEM((tq, aw), BF16),
                        pltpu.VMEM((n_heads, 8, tq), F32),
                        pltpu.VMEM((tq, tq), F32),
                        pltpu.VMEM((tq, tq), BF16),
                        pltpu.VMEM((tq, LANES), F32)],
    )
    return pl.pallas_call(
        functools.partial(_fox_prompt_kernel, n_heads=n_heads, scale=HEAD_DIM ** -0.5, rc=min(64, tq)),
        grid_spec=grid_spec,
        out_shape=jax.ShapeDtypeStruct((n_seq * seq_len, aw), BF16),
        compiler_params=_cparams("parallel", "arbitrary"),
        name="fox_prompt",
    )(qi_tab, kj_tab, q, k_all, v_all, ct)


def _fox_sample_kernel(pt_ref, q_ref, kn_ref, vn_ref, lfnt_ref, *rest, n_heads, n_pg, scale):
    lft_refs = rest[:n_pg]
    k_refs = rest[n_pg:2 * n_pg]
    v_refs = rest[2 * n_pg:3 * n_pg]
    o_ref, m_ref, l_ref, acc_ref, carry_ref, s_ref = rest[3 * n_pg:]
    del pt_ref
    s_idx = pl.program_id(1)
    n_steps = pl.num_programs(1)
    tn = q_ref.shape[0]
    rows = n_pg * n_heads

    @pl.when(s_idx == 0)
    def _():
        m_ref[...] = jnp.full(m_ref.shape, M_INIT, F32)
        l_ref[...] = jnp.zeros_like(l_ref)
        acc_ref[...] = jnp.zeros_like(acc_ref)
        carry_ref[...] = jnp.zeros_like(carry_ref)

    lft = jnp.concatenate([lft_refs[r][...] for r in range(n_pg)], axis=0)
    later = (lax.broadcasted_iota(jnp.int32, (PAGE_SIZE, PAGE_SIZE), 0)
             > lax.broadcasted_iota(jnp.int32, (PAGE_SIZE, PAGE_SIZE), 1)).astype(F32)
    page_sum = _dot_exact(lft, jnp.ones((PAGE_SIZE, LANES), F32))
    ra = lax.broadcasted_iota(jnp.int32, (rows, rows), 0)
    rb = lax.broadcasted_iota(jnp.int32, (rows, rows), 1)
    same_head = (ra & (n_heads - 1)) == (rb & (n_heads - 1))
    later_pages = jnp.where(same_head & (rb > ra), 1.0, 0.0)
    carry = carry_ref[...]
    carry_rows = jnp.concatenate([carry] * n_pg, axis=0)
    rev = (_dot_exact(lft, later) + _dot_exact(later_pages, page_sum) + carry_rows) * LOG2E
    ha = lax.broadcasted_iota(jnp.int32, (n_heads, rows), 0)
    hb = lax.broadcasted_iota(jnp.int32, (n_heads, rows), 1)
    per_head = jnp.where(ha == (hb & (n_heads - 1)), 1.0, 0.0)
    carry_ref[...] = carry + _dot_exact(per_head, page_sum)

    for h in range(n_heads):
        sl = slice(h * HEAD_DIM, (h + 1) * HEAD_DIM)
        qh = q_ref[:, sl].astype(BF16)
        for r in range(n_pg):
            kh = k_refs[r][pl.ds(h, PAGE_SIZE, stride=n_heads), :].astype(BF16)
            row = r * n_heads + h
            s_ref[h * tn:(h + 1) * tn, r * PAGE_SIZE:(r + 1) * PAGE_SIZE] = (
                _dot_nt(qh, kh) * (scale * LOG2E) + rev[row:row + 1, :])
    sc = s_ref[...]
    m_old = m_ref[...]
    m_new = jnp.maximum(m_old, jnp.max(sc, axis=1, keepdims=True))
    alpha = jnp.exp2(m_old - m_new)
    p = jnp.exp2(sc - pltpu.repeat(m_new, n_pg, axis=1))
    l_ref[...] = alpha * l_ref[...] + jnp.sum(p, axis=1, keepdims=True)
    m_ref[...] = m_new
    s_ref[...] = p
    for h in range(n_heads):
        hr = slice(h * tn, (h + 1) * tn)
        acc = alpha[hr, :] * acc_ref[hr, :]
        for r in range(n_pg):
            vh = v_refs[r][pl.ds(h, PAGE_SIZE, stride=n_heads), :].astype(BF16)
            acc = acc + _dot(s_ref[hr, r * PAGE_SIZE:(r + 1) * PAGE_SIZE].astype(BF16), vh)
        acc_ref[hr, :] = acc

    @pl.when(s_idx == n_steps - 1)
    def _():
        upto = (lax.broadcasted_iota(jnp.int32, (LANES, LANES), 0)
                <= lax.broadcasted_iota(jnp.int32, (LANES, LANES), 1)).astype(F32)
        cq_t = _dot_exact(lfnt_ref[...], upto) * LOG2E
        keep = (lax.broadcasted_iota(jnp.int32, (tn, LANES), 1)
                <= lax.broadcasted_iota(jnp.int32, (tn, LANES), 0))
        pad = jnp.zeros((LANES - tn, HEAD_DIM), F32)
        for h in range(n_heads):
            sl = slice(h * HEAD_DIM, (h + 1) * HEAD_DIM)
            hr = slice(h * tn, (h + 1) * tn)
            qh = q_ref[:, sl].astype(BF16)
            kh = jnp.concatenate([kn_ref[:, sl], pad], axis=0).astype(BF16)
            vh = jnp.concatenate([vn_ref[:, sl], pad], axis=0).astype(BF16)
            s = _dot_nt(qh, kh) * (scale * LOG2E) - cq_t[h:h + 1, :]
            s = jnp.where(keep, s, NEG_INF)
            m_o = m_ref[hr, :]
            m_n = jnp.maximum(m_o, jnp.max(s, axis=1, keepdims=True))
            al = jnp.exp2(m_o - m_n)
            pn = jnp.exp2(s - m_n)
            l_n = al * l_ref[hr, :] + jnp.sum(pn, axis=1, keepdims=True)
            o_ref[:, sl] = (al * acc_ref[hr, :] + _dot(pn.astype(BF16), vh)) / l_n


def _fox_sample(q, k_all, v_all, logf_t, cache_k, cache_v, cache_lft, page_table, layer, n_seq, n_new):
    aw = q.shape[1]
    n_heads = aw // HEAD_DIM
    assert n_heads & (n_heads - 1) == 0
    n_pages = page_table.shape[1]
    n_pg = 16 if n_pages % 16 == 0 else (4 if n_pages % 4 == 0 else 1)
    n_steps = n_pages // n_pg

    def page_map(r):
        def f(b, s, pt):
            return (layer, pt[b, (n_steps - 1 - s) * n_pg + r], 0, 0)
        return f

    in_specs = [
        pl.BlockSpec((n_new, aw), lambda b, s, pt: (b, 0)),
        pl.BlockSpec((None, n_new, aw), lambda b, s, pt: (layer, b, 0)),
        pl.BlockSpec((None, n_new, aw), lambda b, s, pt: (layer, b, 0)),
        pl.BlockSpec((None, n_heads, LANES), lambda b, s, pt: (b, 0, 0)),
    ]
    in_specs += [pl.BlockSpec((None, None, n_heads, PAGE_SIZE), page_map(r)) for r in range(n_pg)]
    in_specs += [pl.BlockSpec((None, None, PAGE_SIZE * n_heads, HEAD_DIM), page_map(r))
                 for r in range(n_pg)]
    in_specs += [pl.BlockSpec((None, None, PAGE_SIZE * n_heads, HEAD_DIM), page_map(r))
                 for r in range(n_pg)]
    grid_spec = pltpu.PrefetchScalarGridSpec(
        num_scalar_prefetch=1,
        grid=(n_seq, n_steps),
        in_specs=in_specs,
        out_specs=pl.BlockSpec((n_new, aw), lambda b, s, pt: (b, 0)),
        scratch_shapes=[pltpu.VMEM((n_heads * n_new, LANES), F32),
                        pltpu.VMEM((n_heads * n_new, LANES), F32),
                        pltpu.VMEM((n_heads * n_new, HEAD_DIM), F32),
                        pltpu.VMEM((n_heads, LANES), F32),
                        pltpu.VMEM((n_heads * n_new, n_pg * PAGE_SIZE), F32)],
    )
    args = [page_table, q, k_all, v_all, logf_t] + [cache_lft] * n_pg + [cache_k] * n_pg + [cache_v] * n_pg
    return pl.pallas_call(
        functools.partial(_fox_sample_kernel, n_heads=n_heads, n_pg=n_pg, scale=HEAD_DIM ** -0.5),
        grid_spec=grid_spec,
        out_shape=jax.ShapeDtypeStruct((n_seq * n_new, aw), F32),
        compiler_params=_cparams("parallel", "arbitrary"),
        name="fox_sample",
    )(*args)


def _cmul(a1, a2, x):
    return a1 * x + a2 * pltpu.roll(x, STATE_N, axis=1)


def _chunk_scan(x, ap, n_chunks):
    row = lax.broadcasted_iota(jnp.int32, x.shape, 0)
    hs = x
    sh, lvl = 1, 0
    while sh < n_chunks:
        prev = jnp.where(row >= sh, pltpu.roll(hs, sh, axis=0), 0.0)
        hs = hs + _cmul(ap[2 * lvl:2 * lvl + 1, :], ap[2 * lvl + 1:2 * lvl + 2, :], prev)
        sh *= 2
        lvl += 1
    return hs, jnp.where(row >= 1, pltpu.roll(hs, 1, axis=0), 0.0)


def _ssm_prompt_kernel(s_ref, ws_ref, wo_ref, r_ref, dsk_ref, ap_ref, z_ref, hf_ref,
                       lhs_ref, hb_ref, *, chunk, n_chunks):
    for i in range(chunk):
        lhs_ref[:, i * LANES:(i + 1) * LANES] = s_ref[pl.ds(i, n_chunks, stride=chunk), :].astype(BF16)
    x = _dot(lhs_ref[...], ws_ref[...])
    for g in range(SLAB_GROUPS):
        gs = slice(g * 2 * STATE_N, (g + 1) * 2 * STATE_N)
        hs, hb = _chunk_scan(x[:, gs], ap_ref[g], n_chunks)
        hb_ref[:, gs] = hb.astype(BF16)
        hf_ref[g:g + 1, :] = hs[n_chunks - 1:n_chunks, :]
    hbv = hb_ref[...]
    dsk = dsk_ref[...]
    for i in range(chunk):
        rows = pl.ds(i, n_chunks, stride=chunk)
        y = (_dot(lhs_ref[:, :(i + 1) * LANES], r_ref[(chunk - 1 - i) * LANES:, :])
             + _dot(hbv, wo_ref[i]) + s_ref[rows, :] * dsk)
        z_ref[rows, :] = _gelu_tanh(y)


def _ssm_prompt(s, prm, layer, n_seq, seq_len, chunk):
    sw = s.shape[1]
    n_slabs = sw // LANES
    n_chunks = seq_len // chunk
    ws, wo, rk, dsk, ap = prm
    wq = lambda q, b: (layer, q, 0, 0)
    return pl.pallas_call(
        functools.partial(_ssm_prompt_kernel, chunk=chunk, n_chunks=n_chunks),
        grid=(n_slabs, n_seq),
        in_specs=[
            pl.BlockSpec((seq_len, LANES), lambda q, b: (b, q)),
            pl.BlockSpec((None, None) + ws.shape[2:], wq),
            pl.BlockSpec((None, None) + wo.shape[2:], lambda q, b: (layer, q, 0, 0, 0)),
            pl.BlockSpec((None, None) + rk.shape[2:], wq),
            pl.BlockSpec((None, None, 1, LANES), wq),
            pl.BlockSpec((None, SLAB_GROUPS) + ap.shape[2:], wq),
        ],
        out_specs=[pl.BlockSpec((seq_len, LANES), lambda q, b: (b, q)),
                   pl.BlockSpec((None, SLAB_GROUPS, 2 * STATE_N), lambda q, b: (b, q, 0))],
        out_shape=[jax.ShapeDtypeStruct(s.shape, F32),
                   jax.ShapeDtypeStruct((n_seq, sw // GROUP_CH, 2 * STATE_N), F32)],
        scratch_shapes=[pltpu.VMEM((n_chunks, chunk * LANES), BF16),
                        pltpu.VMEM((n_chunks, SLAB_GROUPS * 2 * STATE_N), BF16)],
        compiler_params=_cparams("parallel", "parallel"),
        name="s5_prompt",
    )(s, ws, wo, rk, dsk, ap)


def _ssm_sample_kernel(u_ref, ws_ref, wo_ref, tp_ref, dsk_ref, ap_ref, h0_ref, z_ref, hf_ref, *, gb):
    for g in range(gb):
        u = u_ref[g]
        ub = u.astype(BF16)
        ap = ap_ref[g]
        hb = h0_ref[g]
        hf_ref[g] = _cmul(ap[0:1, :], ap[1:2, :], hb) + _dot(ub, ws_ref[g])
        y = _dot(ub, tp_ref[g]) + _dot(hb.astype(BF16), wo_ref[g]) + u * dsk_ref[g]
        z_ref[g] = _gelu_tanh(y)


def _ssm_sample(uf, prm, layer, h0):
    n_groups, rows, lc = uf.shape
    gb = 8 if n_groups % 8 == 0 else 1
    g3 = lambda i: (i, 0, 0)
    g4 = lambda i: (layer, i, 0, 0)
    return pl.pallas_call(
        functools.partial(_ssm_sample_kernel, gb=gb),
        grid=(n_groups // gb,),
        in_specs=([pl.BlockSpec((gb, rows, lc), g3)]
                  + [pl.BlockSpec((None, gb) + a.shape[2:], g4) for a in prm]
                  + [pl.BlockSpec((gb, rows, 2 * STATE_N), g3)]),
        out_specs=[pl.BlockSpec((gb, rows, lc), g3),
                   pl.BlockSpec((gb, rows, 2 * STATE_N), g3)],
        out_shape=[jax.ShapeDtypeStruct((n_groups, rows, lc), F32),
                   jax.ShapeDtypeStruct((n_groups, rows, 2 * STATE_N), F32)],
        compiler_params=_cparams("parallel"),
        name="s5_sample",
    )(uf, *prm, h0)


def _ssm_params(lam_re, lam_im, log_dt, b_re, b_im, c_re, c_im, d, chunk, n_chunks):
    hp = dict(precision=HIGHEST)
    dt = jnp.exp(log_dt)[:, None]
    mag = jnp.exp(lam_re * dt)
    a_re = mag * jnp.cos(lam_im * dt)
    a_im = mag * jnp.sin(lam_im * dt)
    den = lam_re * lam_re + lam_im * lam_im
    z_re = ((a_re - 1) * lam_re + a_im * lam_im) / den
    z_im = (a_im * lam_re - (a_re - 1) * lam_im) / den
    bb_re = z_re[..., None] * b_re - z_im[..., None] * b_im
    bb_im = z_re[..., None] * b_im + z_im[..., None] * b_re
    pr, pi = [jnp.ones_like(a_re)], [jnp.zeros_like(a_im)]
    for _ in range(chunk):
        pr.append(pr[-1] * a_re - pi[-1] * a_im)
        pi.append(pr[-2] * a_im + pi[-1] * a_re)
    pw_re = jnp.stack(pr)
    pw_im = jnp.stack(pi)
    dec_re = pw_re[chunk - 1::-1][:chunk]
    dec_im = pw_im[chunk - 1::-1][:chunk]
    ws_re = jnp.einsum('jgn,gnc->gjcn', dec_re, bb_re) - jnp.einsum('jgn,gnc->gjcn', dec_im, bb_im)
    ws_im = jnp.einsum('jgn,gnc->gjcn', dec_re, bb_im) + jnp.einsum('jgn,gnc->gjcn', dec_im, bb_re)
    ws = jnp.concatenate([ws_re, ws_im], axis=-1)
    ca_re = c_re[None] * pw_re[1:, :, None, :] - c_im[None] * pw_im[1:, :, None, :]
    ca_im = c_re[None] * pw_im[1:, :, None, :] + c_im[None] * pw_re[1:, :, None, :]
    wo = jnp.concatenate([ca_re, -ca_im], axis=-1).transpose(1, 3, 0, 2)
    k0_re = c_re[None] * pw_re[:chunk, :, None, :] - c_im[None] * pw_im[:chunk, :, None, :]
    k0_im = c_re[None] * pw_im[:chunk, :, None, :] + c_im[None] * pw_re[:chunk, :, None, :]
    km = (jnp.einsum('mgcn,gnd->gmcd', k0_re, bb_re, **hp)
          - jnp.einsum('mgcn,gnd->gmcd', k0_im, bb_im, **hp))
    lag = np.arange(chunk)[None, :] - np.arange(chunk)[:, None]
    tp = km[:, np.clip(lag, 0, chunk - 1)]
    tp = jnp.where((lag >= 0)[None, :, :, None, None], tp, 0.0).transpose(0, 1, 4, 2, 3)
    qr, qi = pw_re[chunk], pw_im[chunk]
    rows = []
    sh = 1
    while True:
        rows.append(jnp.concatenate([qr, qr], axis=-1))
        rows.append(jnp.concatenate([-qi, qi], axis=-1))
        sh *= 2
        if sh >= n_chunks:
            break
        qr, qi = qr * qr - qi * qi, 2 * qr * qi
    ap = jnp.stack(rows, axis=1)
    pad = (-ap.shape[1]) % 8
    if pad:
        ap = jnp.concatenate([ap, jnp.zeros((ap.shape[0], pad, 2 * STATE_N), F32)], axis=1)
    return ws, wo, tp, d, ap, km


def _group_form(prm, chunk):
    ws, wo, tp, d, ap, _ = prm
    g = ws.shape[0]
    lc = chunk * GROUP_CH
    dsk = jnp.tile(d[:, None, :], (1, chunk, 1)).reshape(g, 1, lc)
    return (ws.reshape(g, lc, 2 * STATE_N).astype(BF16), wo.reshape(g, 2 * STATE_N, lc).astype(BF16),
            tp.reshape(g, lc, lc).astype(BF16), dsk, ap)


def _slab_form(prm, chunk):
    ws, wo, _, d, ap, km = prm
    q = ws.shape[0] // SLAB_GROUPS
    sg, c, n2 = SLAB_GROUPS, GROUP_CH, 2 * STATE_N
    lane_g = np.arange(LANES) // c
    ws_r = ws.reshape(q, sg, chunk, c, n2).transpose(0, 2, 1, 3, 4).reshape(q, chunk * LANES, n2)
    row_g = np.tile(lane_g, chunk)
    ws_s = jnp.concatenate([jnp.where((row_g == h)[None, :, None], ws_r, 0.0) for h in range(sg)],
                           axis=-1).astype(BF16)
    wo_r = wo.reshape(q, sg, n2, chunk, c).transpose(0, 3, 1, 2, 4)
    wo_s = jnp.stack([jnp.pad(wo_r[:, :, g], ((0, 0), (0, 0), (0, 0), (g * c, LANES - (g + 1) * c)))
                      for g in range(sg)], axis=2)
    wo_s = wo_s.reshape(q, chunk, sg * n2, LANES).astype(BF16)
    kt = km.reshape(q, sg, chunk, c, c).transpose(0, 2, 1, 4, 3).reshape(q, chunk, LANES, c)
    blk = jnp.where((lane_g[:, None] == lane_g[None, :])[None, None], jnp.tile(kt, (1, 1, 1, sg)), 0.0)
    rk = blk[:, ::-1].reshape(q, chunk * LANES, LANES).astype(BF16)
    return ws_s, wo_s, rk, d.reshape(q, 1, LANES), ap


def _mix_kernel(h_ref, a_ref, z_ref, ga_ref, gb_ref, wp_ref, wa_ref, wb_ref, wo_ref, *rest):
    o_ref = rest[-1]
    o_att = _dot(a_ref[...].astype(BF16), wp_ref[...])
    zb = z_ref[...].astype(BF16)
    o_ssm = _dot(zb, wa_ref[...]) * jax.nn.sigmoid(_dot(zb, wb_ref[...]))
    g = (ga_ref[...] * o_att + gb_ref[...] * o_ssm).astype(BF16)
    o_ref[...] = h_ref[...] + _dot(g, wo_ref[...])


def _mix(h, a, z, ga, gb, wp, wa, wb, wo, layer, row0, tm, out_buf=None):
    m, d = h.shape
    n_rows = a.shape[0]
    rb0 = row0 // tm
    row = lambda w: pl.BlockSpec((tm, w), lambda i: (i, 0))
    mrow = pl.BlockSpec((tm, d), lambda i: (rb0 + i, 0))
    in_specs = [mrow, row(a.shape[1]), row(z.shape[1]), row(d), row(d),
                _layer_resident(wp, layer), _layer_resident(wa, layer),
                _layer_resident(wb, layer), _layer_resident(wo, layer)]
    args = [h, a, z, ga, gb, wp, wa, wb, wo]
    aliases = {}
    if out_buf is not None:
        aliases[len(args)] = 0
        in_specs.append(pl.BlockSpec(memory_space=pl.ANY))
        args.append(out_buf)
    return pl.pallas_call(
        _mix_kernel,
        grid=(n_rows // tm,),
        in_specs=in_specs,
        out_specs=mrow,
        out_shape=jax.ShapeDtypeStruct((m, d), F32),
        input_output_aliases=aliases,
        compiler_params=_cparams("parallel"),
        name="mix",
    )(*args)


def _xattn_kernel(h_ref, g_ref, wq_ref, qg_ref, mk_ref, mv_ref, wo_ref, o_ref, *, n_heads, scale):
    x = h_ref[...]
    u = _rms(x, g_ref[...]).astype(BF16)
    qx = _dot(u, wq_ref[...])
    outs = []
    for h in range(n_heads):
        sl = slice(h * X_HEAD_DIM, (h + 1) * X_HEAD_DIM)
        qh = _rms(qx[:, sl], qg_ref[...]).astype(BF16)
        s = _dot_nt(qh, mk_ref[:, sl].astype(BF16)) * scale
        e = jnp.exp(s - jnp.max(s, axis=1, keepdims=True))
        p = e / jnp.sum(e, axis=1, keepdims=True)
        outs.append(_dot(p.astype(BF16), mv_ref[:, sl].astype(BF16)))
    o = jnp.concatenate(outs, axis=1).astype(BF16)
    o_ref[...] = x + _dot(o, wo_ref[...])


def _xattn(h, g, wq, qg, mk_all, mv_all, wo, layer, n_seq, seq_len, row0, tm):
    m, d = h.shape
    xw = wq.shape[2]
    n_mem = mk_all.shape[1] // n_seq
    nt = seq_len // tm
    blk0 = row0 // tm
    rows = lambda b, i: (blk0 + b * nt + i, 0)
    small = lambda shape: pl.BlockSpec(shape, lambda b, i: (0,) * len(shape), pipeline_mode=pl.Buffered(1))
    return pl.pallas_call(
        functools.partial(_xattn_kernel, n_heads=xw // X_HEAD_DIM, scale=X_HEAD_DIM ** -0.5),
        grid=(n_seq, nt),
        in_specs=[pl.BlockSpec((tm, d), rows),
                  small((1, d)), _layer_resident(wq, layer), small((1, X_HEAD_DIM)),
                  pl.BlockSpec((None, n_mem, xw), lambda b, i: (layer, b, 0)),
                  pl.BlockSpec((None, n_mem, xw), lambda b, i: (layer, b, 0)),
                  _layer_resident(wo, layer)],
        out_specs=pl.BlockSpec((tm, d), rows),
        out_shape=jax.ShapeDtypeStruct((m, d), F32),
        input_output_aliases={0: 0},
        compiler_params=_cparams("parallel", "arbitrary"),
        name="xattn",
    )(h, g.reshape(1, d), wq, qg.reshape(1, X_HEAD_DIM), mk_all, mv_all, wo)


def kernel(x_prompt, x_sample, cache_k, cache_v, cache_logf, state_ssm_re, state_ssm_im, cache_mem_k, cache_mem_v, page_table, mem_prompt, ffn1_norm, ffn1_w_gate, ffn1_w_up, ffn1_w_down, mix_norm, w_in, b_forget, q_norm, k_norm, ssm_lambda_re, ssm_lambda_im, ssm_log_dt, ssm_b_re, ssm_b_im, ssm_c_re, ssm_c_im, ssm_d, ssm_glu_w, ssm_glu_v, w_att_proj, w_out, cross_norm, mem_norm, w_cq, w_ck, w_cv, cq_norm, ck_norm, w_co, ffn2_norm, ffn2_w_gate, ffn2_w_up, ffn2_w_down):
    bp, t_p, d = x_prompt.shape
    bd, t_s, _ = x_sample.shape
    depth = ffn1_norm.shape[0]
    aw = w_att_proj.shape[1]
    n_heads = aw // HEAD_DIM
    sw = ssm_glu_w.shape[1]
    n_groups = sw // GROUP_CH
    xw = w_cq.shape[2]
    x_heads = xw // X_HEAD_DIM
    n_mem = mem_prompt.shape[1]
    n_pool = cache_k.shape[1]
    mp, ms = bp * t_p, bd * t_s
    mt = mp + ms
    chunk_p = 16
    nc_p = t_p // chunk_p
    tm_p = _row_tile(t_p, 1024, LANES)
    tm_mix = _row_tile(mp, 256)
    assert n_heads <= 8 and t_s == 8 and t_p % chunk_p == 0 and mp % ms == 0 and sw % LANES == 0

    x = jnp.concatenate([x_prompt.reshape(mp, d), x_sample.reshape(ms, d)], axis=0)
    mem = mem_prompt.reshape(bp * n_mem, d)
    ck = cache_k.reshape(depth, n_pool, PAGE_SIZE * n_heads, HEAD_DIM)
    cv = cache_v.reshape(depth, n_pool, PAGE_SIZE * n_heads, HEAD_DIM)
    clft = cache_logf.transpose(0, 1, 3, 2)
    cmk = cache_mem_k.reshape(depth, bd * n_mem, xw)
    cmv = cache_mem_v.reshape(depth, bd * n_mem, xw)

    o_f = 3 * aw
    o_s = o_f + n_heads
    bf = lambda w: w.astype(BF16)
    w_qkv = bf(w_in[:, :, :o_f])
    w_f = bf(jnp.pad(w_in[:, :, o_f:o_s], ((0, 0), (0, 0), (0, LANES - n_heads))))
    w_sg = bf(w_in[:, :, o_s:])
    b_f = jnp.pad(b_forget, ((0, 0), (0, LANES - n_heads)))
    f1g, f1u, f1d = bf(ffn1_w_gate), bf(ffn1_w_up), bf(ffn1_w_down)
    f2g, f2u, f2d = bf(ffn2_w_gate), bf(ffn2_w_up), bf(ffn2_w_down)
    wp_b, wa_b, wb_b, wo_b = bf(w_att_proj), bf(ssm_glu_w), bf(ssm_glu_v), bf(w_out)
    wcq_b, wck_b, wcv_b, wco_b = bf(w_cq), bf(w_ck), bf(w_cv), bf(w_co)

    ssm_raw = (ssm_lambda_re, ssm_lambda_im, ssm_log_dt, ssm_b_re, ssm_b_im, ssm_c_re, ssm_c_im, ssm_d)
    prm_p_all = jax.vmap(lambda *a: _slab_form(_ssm_params(*a, chunk_p, nc_p), chunk_p))(*ssm_raw)
    prm_s_all = jax.vmap(lambda *a: _group_form(_ssm_params(*a, t_s, 1), t_s))(*ssm_raw)

    kp = jnp.zeros((depth, mp, aw), F32)
    vp = jnp.zeros((depth, mp, aw), F32)
    ks = jnp.zeros((depth, ms, aw), F32)
    vs = jnp.zeros((depth, ms, aw), F32)
    mkp = jnp.zeros((depth, bp * n_mem, xw), F32)
    mvp = jnp.zeros((depth, bp * n_mem, xw), F32)
    lf_p_l, lf_s_l, hf_p_l, hf_s_l = [], [], [], []
    for l in range(depth):
        st = lambda buf: buf

        h1, u = _ffn(x, ffn1_norm[l], f1g, f1u, f1d, l, g_next=mix_norm[l])

        qg = q_norm[l].reshape(1, HEAD_DIM)
        kg = k_norm[l].reshape(1, HEAD_DIM)
        pr = dict(row0=0, n_rows=mp, tm=tm_p)
        sr = dict(row0=mp, n_rows=ms, tm=ms)
        (q_p,) = _proj(u, w_qkv, l, 0, aw, mode="headnorm", aux=qg, outs=((BF16, None),), **pr)
        (q_s,) = _proj(u, w_qkv, l, 0, aw, mode="headnorm", aux=qg, **sr)
        (kp,) = _proj(u, w_qkv, l, aw, aw, mode="headnorm", aux=kg, outs=((F32, st(kp)),), **pr)
        (ks,) = _proj(u, w_qkv, l, aw, aw, mode="headnorm", aux=kg, outs=((F32, st(ks)),), **sr)
        (vp,) = _proj(u, w_qkv, l, 2 * aw, aw, outs=((F32, st(vp)),), **pr)
        (vs,) = _proj(u, w_qkv, l, 2 * aw, aw, outs=((F32, st(vs)),), **sr)
        (lf_p,) = _proj(u, w_f, l, 0, LANES, mode="logsigmoid", aux=b_f[l:l + 1], **pr)
        (lf_s,) = _proj(u, w_f, l, 0, LANES, mode="logsigmoid", aux=b_f[l:l + 1], **sr)
        (s_p,) = _proj(u, w_sg, l, 0, sw, **pr)
        (s_s,) = _proj(u, w_sg, l, 0, sw, **sr)
        (ga_p,) = _proj(u, w_sg, l, sw, d, mode="sigmoid", **pr)
        (ga_s,) = _proj(u, w_sg, l, sw, d, mode="sigmoid", **sr)
        (gb_p,) = _proj(u, w_sg, l, sw + d, d, mode="sigmoid", **pr)
        (gb_s,) = _proj(u, w_sg, l, sw + d, d, mode="sigmoid", **sr)

        ct = _cumsum(lf_p, bp, t_p)
        att_p = _fox_prompt(q_p, kp, vp, ct, l, bp, t_p)
        lf_st = lf_s.reshape(bd, t_s, LANES)[:, :, :n_heads].transpose(0, 2, 1)
        lf_st = jnp.pad(lf_st, ((0, 0), (0, 0), (0, LANES - t_s)))
        att_s = _fox_sample(q_s, ks, vs, lf_st, ck, cv, clft, page_table, l, bd, t_s)

        z_p, hf_p = _ssm_prompt(s_p, prm_p_all, l, bp, t_p, chunk_p)
        uf_s = (s_s.reshape(bd, t_s, n_groups, GROUP_CH).transpose(2, 0, 1, 3)
                .reshape(n_groups, bd, t_s * GROUP_CH))
        h0 = jnp.concatenate([state_ssm_re[l], state_ssm_im[l]], axis=-1).transpose(1, 0, 2)
        zf_s, hf_s = _ssm_sample(uf_s, prm_s_all, l, h0)
        z_s = zf_s.reshape(n_groups, bd, t_s, GROUP_CH).transpose(1, 2, 0, 3).reshape(ms, sw)

        h2 = _mix(h1, att_p, z_p, ga_p, gb_p, wp_b, wa_b, wb_b, wo_b, l, 0, tm_mix)
        h2 = _mix(h1, att_s, z_s, ga_s, gb_s, wp_b, wa_b, wb_b, wo_b, l, mp, ms, out_buf=h2)

        mr = dict(row0=0, n_rows=bp * n_mem, tm=_row_tile(bp * n_mem, 512), gain_in=mem_norm[l])
        (mkp,) = _proj(mem, wck_b, l, 0, xw, mode="headnorm", aux=ck_norm[l].reshape(1, X_HEAD_DIM),
                       outs=((F32, st(mkp)),), **mr)
        (mvp,) = _proj(mem, wcv_b, l, 0, xw, outs=((F32, st(mvp)),), **mr)
        h3 = _xattn(h2, cross_norm[l], wcq_b, cq_norm[l], mkp, mvp, wco_b, l, bp, t_p, 0,
                    _row_tile(t_p, 512, LANES))
        h3 = _xattn(h3, cross_norm[l], wcq_b, cq_norm[l], cmk, cmv, wco_b, l, bd, t_s, mp, t_s)

        x = _ffn(h3, ffn2_norm[l], f2g, f2u, f2d, l)

        lf_p_l.append(lf_p)
        lf_s_l.append(lf_s)
        hf_p_l.append(hf_p)
        hf_s_l.append(hf_s)

    lf_p_all = jnp.stack(lf_p_l)[:, :, :n_heads]
    lf_s_all = jnp.stack(lf_s_l)[:, :, :n_heads]
    hf_p_all = jnp.stack(hf_p_l)
    hf_s_all = jnp.stack(hf_s_l).transpose(0, 2, 1, 3)
    return (x[:mp].reshape(bp, t_p, d), x[mp:].reshape(bd, t_s, d),
            kp.reshape(depth, bp, t_p, n_heads, HEAD_DIM), vp.reshape(depth, bp, t_p, n_heads, HEAD_DIM),
            lf_p_all.reshape(depth, bp, t_p, n_heads),
            hf_p_all[..., :STATE_N], hf_p_all[..., STATE_N:],
            mkp.reshape(depth, bp, n_mem, x_heads, X_HEAD_DIM), mvp.reshape(depth, bp, n_mem, x_heads, X_HEAD_DIM),
            ks.reshape(depth, bd, t_s, n_heads, HEAD_DIM), vs.reshape(depth, bd, t_s, n_heads, HEAD_DIM),
            lf_s_all.reshape(depth, bd, t_s, n_heads),
            hf_s_all[..., :STATE_N], hf_s_all[..., STATE_N:])
```
